```python
import math
import jax, jax.numpy as jnp
from jax import lax
import numpy as np

D_MODEL = 2048
BATCH = 4
SEQ = 2048
DEPTH = 1
DEC_BATCH = 32
DEC_SEQ = 8
PAST_LEN = 16384
PAGE_SIZE = 128

N_META = 16
D_MIX = D_MODEL
D_ATTN = D_MIX // 2
D_CONV = D_MIX - D_ATTN
HEAD_DIM = 64
N_HEADS = D_ATTN // (2 * HEAD_DIM)
QK_DIM = 2 * HEAD_DIM
V_DIM = 2 * HEAD_DIM
CONV_WIDTH = 31
D_FF = 5632
EPS = 1e-6
Q_BLOCK = 128
POOL_NUM, POOL_DEN = 5, 4

kernel_name = "hymba_diffattn_conformer_macaron_step"


def rms_norm(x, g):
    xf = x.astype(jnp.float32)
    y = xf * lax.rsqrt(jnp.mean(xf * xf, axis=-1, keepdims=True) + EPS)
    return (y * g.astype(jnp.float32)).astype(x.dtype)


def layer_norm(x, g, b):
    xf = x.astype(jnp.float32)
    mu = jnp.mean(xf, axis=-1, keepdims=True)
    var = jnp.mean(jnp.square(xf - mu), axis=-1, keepdims=True)
    y = (xf - mu) * lax.rsqrt(var + EPS) * g.astype(jnp.float32) + b.astype(jnp.float32)
    return y.astype(x.dtype)


def half_ffn(x, g, w_gate, w_up, w_down):
    h = rms_norm(x, g)
    return x + 0.5 * ((jax.nn.silu(h @ w_gate) * (h @ w_up)) @ w_down)


def alibi_slopes(n):
    return jnp.array([2.0 ** (-8.0 * (i + 1) / n) for i in range(n)], dtype=jnp.float32)


def lambda_init_fn(layer):
    return 0.8 - 0.6 * math.exp(-0.3 * layer)


def diff_lambda(lq1, lk1, lq2, lk2, lam_init):
    dot_exp = lambda a, b: jnp.exp(jnp.sum(a.astype(jnp.float32) * b.astype(jnp.float32)))
    return dot_exp(lq1, lk1) - dot_exp(lq2, lk2) + lam_init


def mixer_projections(h, w_in):
    u = h @ w_in
    q, k, v, a, g = jnp.split(u, [D_ATTN, 2 * D_ATTN, 3 * D_ATTN, 3 * D_ATTN + D_CONV], axis=-1)
    B, T = h.shape[:2]
    q = q.reshape(B, T, N_HEADS, QK_DIM)
    k = k.reshape(B, T, N_HEADS, QK_DIM)
    v = v.reshape(B, T, N_HEADS, V_DIM)
    glu = a * jax.nn.sigmoid(g)
    return q, k, v, glu


def two_map_scores(q1, q2, kb, qpos, kpos, slopes):
    s1 = jnp.einsum('bqhd,bkhd->bhqk', q1, kb[..., :HEAD_DIM])
    s2 = jnp.einsum('bqhd,bkhd->bhqk', q2, kb[..., HEAD_DIM:])
    dist = qpos[:, None] - kpos[None, :]
    bias = -slopes[:, None, None] * dist.astype(jnp.float32)[None]
    s = jnp.stack([s1, s2]).astype(jnp.float32) + bias
    return jnp.where(dist >= 0, s, -jnp.inf)


def head_group_out(o, subln_g, lam_init):
    B, T = o.shape[:2]
    return (rms_norm(o, subln_g) * (1.0 - lam_init)).reshape(B, T, D_ATTN)


def prompt_diff_attention(q, k, v, lam, slopes):
    B, T = q.shape[:2]
    n_blk = -(-T // Q_BLOCK)
    t_pad = n_blk * Q_BLOCK
    pad = ((0, 0), (0, t_pad - T), (0, 0), (0, 0))
    qp = jnp.pad(q, pad) * (HEAD_DIM ** -0.5)
    kp = jnp.pad(k, pad)
    vp = jnp.pad(v, pad)
    kpos = jnp.arange(t_pad)

    def block(i):
        qb = lax.dynamic_slice_in_dim(qp, i * Q_BLOCK, Q_BLOCK, axis=1)
        qpos = i * Q_BLOCK + jnp.arange(Q_BLOCK)
        s = two_map_scores(qb[..., :HEAD_DIM], qb[..., HEAD_DIM:], kp, qpos, kpos, slopes)
        p = jax.nn.softmax(s, axis=-1)
        pd = p[0] - lam * p[1]
        return jnp.einsum('bhqk,bkhd->bqhd', pd.astype(vp.dtype), vp)

    out = lax.map(block, jnp.arange(n_blk))
    out = jnp.transpose(out, (1, 0, 2, 3, 4)).reshape(B, t_pad, N_HEADS, V_DIM)
    return out[:, :T]


def online_update(carry, s, vb):
    m, l, acc = carry
    m_new = jnp.maximum(m, jnp.max(s, axis=-1))
    corr = jnp.exp(m - m_new)
    p = jnp.exp(s - m_new[..., None])
    l = l * corr + jnp.sum(p, axis=-1)
    acc = acc * corr[..., None] + jnp.einsum('mbhqk,bkhd->mbhqd', p, vb.astype(jnp.float32))
    return (m_new, l, acc)


def sample_diff_attention(q, k_new, v_new, cache_k, cache_v, layer, page_table, lam, slopes):
    Bd, S = q.shape[:2]
    n_pages = page_table.shape[1]
    past = n_pages * PAGE_SIZE
    qs = q * (HEAD_DIM ** -0.5)
    q1, q2 = qs[..., :HEAD_DIM], qs[..., HEAD_DIM:]
    qpos = past + jnp.arange(S)
    init = (jnp.full((2, Bd, N_HEADS, S), -jnp.inf, jnp.float32),
            jnp.zeros((2, Bd, N_HEADS, S), jnp.float32),
            jnp.zeros((2, Bd, N_HEADS, S, V_DIM), jnp.float32))

    def page_step(carry, p):
        phys = page_table[:, p]
        kb = cache_k[layer, phys]
        vb = cache_v[layer, phys]
        kpos = p * PAGE_SIZE + jnp.arange(PAGE_SIZE)
        return online_update(carry, two_map_scores(q1, q2, kb, qpos, kpos, slopes), vb), None

    carry, _ = lax.scan(page_step, init, jnp.arange(n_pages))
    carry = online_update(carry, two_map_scores(q1, q2, k_new, qpos, qpos, slopes), v_new)
    m, l, acc = carry
    o = acc / l[..., None]
    diff = o[0] - lam * o[1]
    return jnp.transpose(diff, (0, 2, 1, 3)).astype(q.dtype)


def depthwise_conv_valid(x_ext, w, b):
    out = lax.conv_general_dilated(x_ext, w[:, None, :].astype(x_ext.dtype), window_strides=(1,),
                                   padding='VALID', dimension_numbers=('NWC', 'WIO', 'NWC'),
                                   feature_group_count=D_CONV)
    return out + b


def conv_group_tail(y, g, b):
    return jax.nn.silu(layer_norm(y, g, b))


def setup_inputs(seed: int = 0) -> dict:
    key = jax.random.key(seed)
    ks = jax.random.split(key, 32)
    n_pages = PAST_LEN // PAGE_SIZE
    n_pool = (DEC_BATCH * n_pages * POOL_NUM) // POOL_DEN
    f32 = jnp.float32
    nrm = lambda k, shape, s=1.0: jax.random.normal(k, shape, f32) * s
    gain = lambda k, shape: 1.0 + 0.02 * jax.random.normal(k, shape, f32)
    page_table = jax.random.permutation(ks[5], n_pool)[:DEC_BATCH * n_pages]
    page_table = page_table.reshape(DEC_BATCH, n_pages).astype(jnp.int32)
    d_in = 3 * D_ATTN + 2 * D_CONV
    return {
        "x_prompt": nrm(ks[0], (BATCH, SEQ, D_MODEL)),
        "x_sample": nrm(ks[1], (DEC_BATCH, DEC_SEQ, D_MODEL)),
        "cache_k": nrm(ks[2], (DEPTH, n_pool, PAGE_SIZE, N_HEADS, QK_DIM)),
        "cache_v": nrm(ks[3], (DEPTH, n_pool, PAGE_SIZE, N_HEADS, V_DIM)),
        "state_conv": nrm(ks[4], (DEPTH, DEC_BATCH, CONV_WIDTH - 1, D_CONV), 0.5),
        "page_table": page_table,
        "meta_tokens": nrm(ks[6], (N_META, D_MODEL)),
        "ffn1_norm": gain(ks[7], (DEPTH, D_MODEL)),
        "ffn1_w_gate": nrm(ks[8], (DEPTH, D_MODEL, D_FF), D_MODEL ** -0.5),
        "ffn1_w_up": nrm(ks[9], (DEPTH, D_MODEL, D_FF), D_MODEL ** -0.5),
        "ffn1_w_down": nrm(ks[10], (DEPTH, D_FF, D_MODEL), D_FF ** -0.5),
        "mix_norm": gain(ks[11], (DEPTH, D_MODEL)),
        "w_in": nrm(ks[12], (DEPTH, D_MODEL, d_in), D_MODEL ** -0.5),
        "lambda_q1": nrm(ks[13], (DEPTH, HEAD_DIM), 0.1),
        "lambda_k1": nrm(ks[14], (DEPTH, HEAD_DIM), 0.1),
        "lambda_q2": nrm(ks[15], (DEPTH, HEAD_DIM), 0.1),
        "lambda_k2": nrm(ks[16], (DEPTH, HEAD_DIM), 0.1),
        "attn_subln": gain(ks[17], (DEPTH, V_DIM)),
        "conv_w": nrm(ks[18], (DEPTH, CONV_WIDTH, D_CONV), CONV_WIDTH ** -0.5),
        "conv_b": nrm(ks[19], (DEPTH, D_CONV), 0.02),
        "conv_norm_g": gain(ks[20], (DEPTH, D_CONV)),
        "conv_norm_b": nrm(ks[21], (DEPTH, D_CONV), 0.02),
        "w_out": nrm(ks[22], (DEPTH, D_MIX, D_MODEL), D_MIX ** -0.5),
        "ffn2_norm": gain(ks[23], (DEPTH, D_MODEL)),
        "ffn2_w_gate": nrm(ks[24], (DEPTH, D_MODEL, D_FF), D_MODEL ** -0.5),
        "ffn2_w_up": nrm(ks[25], (DEPTH, D_MODEL, D_FF), D_MODEL ** -0.5),
        "ffn2_w_down": nrm(ks[26], (DEPTH, D_FF, D_MODEL), D_FF ** -0.5),
        "final_norm": gain(ks[27], (D_MODEL,)),
    }


def reference(x_prompt, x_sample, cache_k, cache_v, state_conv, page_table, meta_tokens,
              ffn1_norm, ffn1_w_gate, ffn1_w_up, ffn1_w_down, mix_norm, w_in,
              lambda_q1, lambda_k1, lambda_q2, lambda_k2, attn_subln,
              conv_w, conv_b, conv_norm_g, conv_norm_b, w_out,
              ffn2_norm, ffn2_w_gate, ffn2_w_up, ffn2_w_down, final_norm):
    slopes = alibi_slopes(N_HEADS)
    B = x_prompt.shape[0]
    meta = jnp.broadcast_to(meta_tokens.astype(x_prompt.dtype)[None], (B, N_META, D_MODEL))
    xp = jnp.concatenate([meta, x_prompt], axis=1)
    xs = x_sample
    k_p_l, v_p_l, c_p_l, k_s_l, v_s_l, c_s_l = [], [], [], [], [], []
    for l in range(DEPTH):
        lam_init = lambda_init_fn(l)
        lam = diff_lambda(lambda_q1[l], lambda_k1[l], lambda_q2[l], lambda_k2[l], lam_init)

        xp = half_ffn(xp, ffn1_norm[l], ffn1_w_gate[l], ffn1_w_up[l], ffn1_w_down[l])
        xs = half_ffn(xs, ffn1_norm[l], ffn1_w_gate[l], ffn1_w_up[l], ffn1_w_down[l])

        qp, kp, vp, glup = mixer_projections(rms_norm(xp, mix_norm[l]), w_in[l])
        att_p = head_group_out(prompt_diff_attention(qp, kp, vp, lam, slopes), attn_subln[l], lam_init)
        glup_ext = jnp.pad(glup, ((0, 0), (CONV_WIDTH - 1, 0), (0, 0)))
        conv_p = conv_group_tail(depthwise_conv_valid(glup_ext, conv_w[l], conv_b[l]),
                                 conv_norm_g[l], conv_norm_b[l])
        xp = xp + jnp.concatenate([att_p, conv_p], axis=-1) @ w_out[l]
        k_p_l.append(kp)
        v_p_l.append(vp)
        c_p_l.append(glup_ext[:, -(CONV_WIDTH - 1):])

        qs, ks_, vs, glus = mixer_projections(rms_norm(xs, mix_norm[l]), w_in[l])
        att_s = head_group_out(sample_diff_attention(qs, ks_, vs, cache_k, cache_v, l, page_table, lam, slopes),
                               attn_subln[l], lam_init)
        glus_ext = jnp.concatenate([state_conv[l].astype(glus.dtype), glus], axis=1)
        conv_s = conv_group_tail(depthwise_conv_valid(glus_ext, conv_w[l], conv_b[l]),
                                 conv_norm_g[l], conv_norm_b[l])
        xs = xs + jnp.concatenate([att_s, conv_s], axis=-1) @ w_out[l]
        k_s_l.append(ks_)
        v_s_l.append(vs)
        c_s_l.append(glus_ext[:, -(CONV_WIDTH - 1):])

        xp = half_ffn(xp, ffn2_norm[l], ffn2_w_gate[l], ffn2_w_up[l], ffn2_w_down[l])
        xs = half_ffn(xs, ffn2_norm[l], ffn2_w_gate[l], ffn2_w_up[l], ffn2_w_down[l])

    y_prompt = rms_norm(xp, final_norm)[:, N_META:]
    y_sample = rms_norm(xs, final_norm)
    k_prompt = jnp.stack(k_p_l)
    v_prompt = jnp.stack(v_p_l)
    conv_prompt = jnp.stack(c_p_l)
    k_sample = jnp.stack(k_s_l)
    v_sample = jnp.stack(v_s_l)
    conv_sample = jnp.stack(c_s_l)
    return (y_prompt, y_sample, k_prompt, v_prompt, conv_prompt, k_sample, v_sample, conv_sample)
```

```python
import functools
import math

import jax
import jax.numpy as jnp
from jax import lax
from jax.experimental import pallas as pl
from jax.experimental.pallas import tpu as pltpu

EPS = 1e-6
HEAD_DIM = 64
HEAD_WIDTH = 2 * HEAD_DIM
LANES = 128
SUBLANES = 8
BF16_ROWS = 16
VMEM_LIMIT_BYTES = 56 * 1024 * 1024

BF16 = jnp.bfloat16
F32 = jnp.float32
_NT = (((1,), (1,)), ((), ()))


def _row_tile(rows, target):
    best = None
    for t in range(BF16_ROWS, min(rows, target) + 1, BF16_ROWS):
        if rows % t == 0:
            best = t
    assert best is not None, (rows, target)
    return best


def _col_tile(cols, target):
    best = None
    for t in range(LANES, min(cols, target) + 1, LANES):
        if cols % t == 0:
            best = t
    assert best is not None, (cols, target)
    return best


def _idiv(x, n):
    if n & (n - 1) == 0:
        return lax.shift_right_logical(x, n.bit_length() - 1)
    return x // n


def _imod(x, n):
    if n & (n - 1) == 0:
        return lax.bitwise_and(x, n - 1)
    return lax.rem(x, n)


def _rms(x, gain):
    return x * lax.rsqrt(jnp.mean(x * x, axis=-1, keepdims=True) + EPS) * gain


def _params(*semantics):
    return pltpu.CompilerParams(dimension_semantics=semantics, vmem_limit_bytes=VMEM_LIMIT_BYTES)


def _ffn_kernel(*refs, final):
    if final:
        x_ref, g_ref, wg_ref, wu_ref, wd_ref, fg_ref, o_ref, h_ref = refs
    else:
        x_ref, g_ref, wg_ref, wu_ref, wd_ref, o_ref, h_ref = refs
    j = pl.program_id(1)

    @pl.when(j == 0)
    def _():
        h_ref[...] = _rms(x_ref[...], g_ref[...]).astype(BF16)
        o_ref[...] = jnp.zeros_like(o_ref)

    h = h_ref[...]
    gate = jnp.dot(h, wg_ref[...], preferred_element_type=F32)
    up = jnp.dot(h, wu_ref[...], preferred_element_type=F32)
    act = (gate * jax.nn.sigmoid(gate) * up).astype(BF16)
    o_ref[...] += jnp.dot(act, wd_ref[...], preferred_element_type=F32)

    @pl.when(j == pl.num_programs(1) - 1)
    def _():
        y = x_ref[...] + 0.5 * o_ref[...]
        if final:
            y = _rms(y, fg_ref[...])
        o_ref[...] = y


def _ffn(x, gain, w_gate, w_up, w_down, final_gain=None):
    rows, d = x.shape
    f = w_gate.shape[1]
    tm = _row_tile(rows, 704)
    tf = _col_tile(f, 512)
    final = final_gain is not None
    in_specs = [
        pl.BlockSpec((tm, d), lambda i, j: (i, 0)),
        pl.BlockSpec((1, d), lambda i, j: (0, 0)),
        pl.BlockSpec((d, tf), lambda i, j: (0, j)),
        pl.BlockSpec((d, tf), lambda i, j: (0, j)),
        pl.BlockSpec((tf, d), lambda i, j: (j, 0)),
    ]
    args = [x, gain.reshape(1, d), w_gate, w_up, w_down]
    if final:
        in_specs.append(pl.BlockSpec((1, d), lambda i, j: (0, 0)))
        args.append(final_gain.reshape(1, d))
    return pl.pallas_call(
        functools.partial(_ffn_kernel, final=final),
        grid=(rows // tm, f // tf),
        in_specs=in_specs,
        out_specs=pl.BlockSpec((tm, d), lambda i, j: (i, 0)),
        out_shape=jax.ShapeDtypeStruct((rows, d), F32),
        scratch_shapes=[pltpu.VMEM((tm, d), BF16)],
        compiler_params=_params("parallel", "arbitrary"),
        name="half_ffn_final" if final else "half_ffn",
    )(*args)


def _mix_in_kernel(x_ref, g_ref, wa_ref, wg_ref, q_ref, k_ref, v_ref, glu_ref, h_ref):
    j = pl.program_id(1)

    @pl.when(j == 0)
    def _():
        h_ref[...] = _rms(x_ref[...], g_ref[...]).astype(BF16)

    h = h_ref[...]
    r = jnp.dot(h, wa_ref[...], preferred_element_type=F32)

    @pl.when(j == 0)
    def _():
        q_ref[...] = r * (HEAD_DIM ** -0.5)

    @pl.when(j == 1)
    def _():
        k_ref[...] = r

    @pl.when(j == 2)
    def _():
        v_ref[...] = r

    @pl.when(j == 3)
    def _():
        gate = jnp.dot(h, wg_ref[...], preferred_element_type=F32)
        glu_ref[...] = r * jax.nn.sigmoid(gate)


def _mix_in(x, gain, w_in, width):
    rows, d = x.shape
    assert w_in.shape[1] == 5 * width
    tm = _row_tile(rows, 704)
    out_spec = pl.BlockSpec((tm, width), lambda i, j: (i, 0))
    out_sds = jax.ShapeDtypeStruct((rows, width), F32)
    return pl.pallas_call(
        _mix_in_kernel,
        grid=(rows // tm, 4),
        in_specs=[
            pl.BlockSpec((tm, d), lambda i, j: (i, 0)),
            pl.BlockSpec((1, d), lambda i, j: (0, 0)),
            pl.BlockSpec((d, width), lambda i, j: (0, j)),
            pl.BlockSpec((d, width), lambda i, j: (0, 4)),
        ],
        out_specs=[out_spec] * 4,
        out_shape=[out_sds] * 4,
        scratch_shapes=[pltpu.VMEM((tm, d), BF16)],
        compiler_params=_params("parallel", "arbitrary"),
        name="mix_in",
    )(x, gain.reshape(1, d), w_in, w_in)


def _diff_lambda(lam_ref, lam_init):
    lv = lam_ref[...]
    d1 = jnp.sum(lv[0:1] * lv[1:2], axis=-1, keepdims=True)
    d2 = jnp.sum(lv[2:3] * lv[3:4], axis=-1, keepdims=True)
    return jnp.exp(d1) - jnp.exp(d2) + lam_init


def _attn_prompt_kernel(slopes_ref, q_ref, k_ref, v_ref, lam_ref, sub_ref, o_ref,
                        kb_ref, vb_ref, dist_ref, m_ref, l_ref, acc_ref,
                        *, seq, tq, tk, lam_init):
    head = pl.program_id(1)
    slope = slopes_ref[head]
    seq_pad = kb_ref.shape[0]

    kb_ref[0:seq, :] = k_ref[0].astype(BF16)
    vb_ref[0:seq, :] = v_ref[0].astype(BF16)
    if seq_pad > seq:
        kb_ref[seq:seq_pad, :] = jnp.zeros((seq_pad - seq, HEAD_WIDTH), BF16)
        vb_ref[seq:seq_pad, :] = jnp.zeros((seq_pad - seq, HEAD_WIDTH), BF16)
    dist_ref[...] = (lax.broadcasted_iota(jnp.int32, (tq, tk), 1)
                     - lax.broadcasted_iota(jnp.int32, (tq, tk), 0))
    lam = _diff_lambda(lam_ref, lam_init)
    sub_gain = sub_ref[...] * (1.0 - lam_init)

    def q_tile(i, carry):
        q0 = pl.multiple_of(i * tq, SUBLANES)
        qt = q_ref[0, pl.ds(q0, tq), :]
        qs = (qt[:, :HEAD_DIM].astype(BF16), qt[:, HEAD_DIM:].astype(BF16))
        m_ref[...] = jnp.full(m_ref.shape, -jnp.inf, F32)
        l_ref[...] = jnp.zeros_like(l_ref)
        acc_ref[...] = jnp.zeros_like(acc_ref)

        def kv_tile(j, carry, masked):
            k0 = pl.multiple_of(j * tk, tk)
            kt = kb_ref[pl.ds(k0, tk), :]
            vt = vb_ref[pl.ds(k0, tk), :]
            off = q0 - k0
            dist = dist_ref[...]
            bias = slope * (dist - off).astype(F32)
            for mp in range(2):
                kmp = kt[:, mp * HEAD_DIM:(mp + 1) * HEAD_DIM]
                s = lax.dot_general(qs[mp], kmp, _NT, preferred_element_type=F32) + bias
                if masked:
                    s = jnp.where(dist <= off, s, -jnp.inf)
                m_old = m_ref[mp]
                m_new = jnp.maximum(m_old, jnp.max(s, axis=-1, keepdims=True))
                corr = jnp.exp(m_old - m_new)
                p = jnp.exp(s - m_new)
                l_ref[mp] = l_ref[mp] * corr + jnp.sum(p, axis=-1, keepdims=True)
                acc_ref[mp] = acc_ref[mp] * corr + jnp.dot(
                    p.astype(BF16), vt, preferred_element_type=F32)
                m_ref[mp] = m_new
            return carry

        n_full = q0 // tk
        n_all = (q0 + tq + tk - 1) // tk
        lax.fori_loop(0, n_full, functools.partial(kv_tile, masked=False), 0)
        lax.fori_loop(n_full, n_all, functools.partial(kv_tile, masked=True), 0)

        o1 = acc_ref[0] / l_ref[0]
        o2 = acc_ref[1] / l_ref[1]
        o_ref[0, pl.ds(q0, tq), :] = _rms(o1 - lam * o2, sub_gain)
        return carry

    lax.fori_loop(0, seq // tq, q_tile, 0)


def _seq_tile(seq, target):
    best = None
    for t in range(SUBLANES, min(seq, target) + 1, SUBLANES):
        if seq % t == 0:
            best = t
    assert best is not None, (seq, target)
    return best


def _attn_prompt(q, k, v, lam_vecs, subln, slopes, lam_init):
    batch, seq, width = q.shape
    heads = width // HEAD_WIDTH
    tq = _seq_tile(seq, 384)
    tk = 256
    seq_pad = -(-seq // tk) * tk
    blk = pl.BlockSpec((1, seq, HEAD_WIDTH), lambda b, h: (b, 0, h))
    return pl.pallas_call(
        functools.partial(_attn_prompt_kernel, seq=seq, tq=tq, tk=tk, lam_init=lam_init),
        grid=(batch, heads),
        in_specs=[
            pl.BlockSpec(memory_space=pltpu.SMEM),
            blk, blk, blk,
            pl.BlockSpec((4, HEAD_DIM), lambda b, h: (0, 0)),
            pl.BlockSpec((1, HEAD_WIDTH), lambda b, h: (0, 0)),
        ],
        out_specs=blk,
        out_shape=jax.ShapeDtypeStruct((batch, seq, width), F32),
        scratch_shapes=[
            pltpu.VMEM((seq_pad, HEAD_WIDTH), BF16),
            pltpu.VMEM((seq_pad, HEAD_WIDTH), BF16),
            pltpu.VMEM((tq, tk), jnp.int32),
            pltpu.VMEM((2, tq, 1), F32),
            pltpu.VMEM((2, tq, 1), F32),
            pltpu.VMEM((2, tq, HEAD_WIDTH), F32),
        ],
        compiler_params=_params("parallel", "parallel"),
        name="attn_prompt",
    )(slopes, q, k, v, lam_vecs, subln.reshape(1, HEAD_WIDTH))


def _attn_sample_kernel(pt_ref, slopes_ref, q_ref, kn_ref, vn_ref, lam_ref, sub_ref, *rest,
                        pages, heads, past, lam_init):
    k_refs = rest[:pages]
    v_refs = rest[pages:2 * pages]
    o_ref, qt_ref, b0_ref, sl_ref, m_ref, l_ref, acc_ref, kc_ref, vc_ref = rest[2 * pages:]
    del pt_ref
    c = pl.program_id(1)
    n_tok = q_ref.shape[1]
    rows = qt_ref.shape[0]
    group = 2 * n_tok
    page = k_refs[0].shape[1] // heads
    chunk = pages * page

    def head_rows(h):
        return slice(h * group, (h + 1) * group)

    def head_cols(h):
        return slice(h * HEAD_WIDTH, (h + 1) * HEAD_WIDTH)

    @pl.when(c == 0)
    def _():
        q = q_ref[0]
        first_half = lax.broadcasted_iota(jnp.int32, (n_tok, HEAD_WIDTH), 1) < HEAD_DIM
        for h in range(heads):
            qh = q[:, head_cols(h)]
            blk = jnp.concatenate([jnp.where(first_half, qh, 0.0), jnp.where(first_half, 0.0, qh)], axis=0)
            qt_ref[head_rows(h), :] = blk.astype(BF16)
        row = lax.broadcasted_iota(jnp.int32, (rows, 1), 0)
        slope_col = jnp.zeros((rows, 1), F32)
        for h in range(heads):
            slope_col = jnp.where(_idiv(row, group) == h, slopes_ref[h], slope_col)
        sl_ref[...] = slope_col
        lane = lax.broadcasted_iota(jnp.int32, (rows, chunk), 1)
        tok = _imod(lax.broadcasted_iota(jnp.int32, (rows, chunk), 0), n_tok)
        b0_ref[...] = slope_col * (lane - tok).astype(F32)
        m_ref[...] = jnp.full(m_ref.shape, -jnp.inf, F32)
        l_ref[...] = jnp.zeros_like(l_ref)
        acc_ref[...] = jnp.zeros_like(acc_ref)

    def scores(n_keys):
        return jnp.concatenate(
            [lax.dot_general(qt_ref[head_rows(h), :], kc_ref[h, 0:n_keys, :], _NT,
                             preferred_element_type=F32) for h in range(heads)], axis=0)

    def update(s, n_keys):
        m_old = m_ref[...]
        m_new = jnp.maximum(m_old, jnp.max(s, axis=-1, keepdims=True))
        corr = jnp.exp(m_old - m_new)
        p = jnp.exp(s - m_new)
        l_ref[...] = l_ref[...] * corr + jnp.sum(p, axis=-1, keepdims=True)
        pb = p.astype(BF16)
        for h in range(heads):
            rs = head_rows(h)
            acc_ref[rs, :] = acc_ref[rs, :] * corr[rs] + jnp.dot(
                pb[rs], vc_ref[h, 0:n_keys, :], preferred_element_type=F32)
        m_ref[...] = m_new

    for p in range(pages):
        for h in range(heads):
            kc_ref[h, p * page:(p + 1) * page, :] = k_refs[p][0, pl.ds(h, page, stride=heads), :].astype(BF16)
            vc_ref[h, p * page:(p + 1) * page, :] = v_refs[p][0, pl.ds(h, page, stride=heads), :].astype(BF16)
    row_shift = sl_ref[...] * (c * chunk - past).astype(F32)
    update(scores(chunk) + b0_ref[...] + row_shift, chunk)

    @pl.when(c == pl.num_programs(1) - 1)
    def _():
        pad = jnp.zeros((page - n_tok, HEAD_WIDTH), F32)
        for h in range(heads):
            kc_ref[h, 0:page, :] = jnp.concatenate([kn_ref[0][:, head_cols(h)], pad], axis=0).astype(BF16)
            vc_ref[h, 0:page, :] = jnp.concatenate([vn_ref[0][:, head_cols(h)], pad], axis=0).astype(BF16)
        lane = lax.broadcasted_iota(jnp.int32, (rows, page), 1)
        tok = _imod(lax.broadcasted_iota(jnp.int32, (rows, page), 0), n_tok)
        update(jnp.where(lane <= tok, scores(page) + b0_ref[:, 0:page], -jnp.inf), page)

        lam = _diff_lambda(lam_ref, lam_init)
        sub_gain = sub_ref[...] * (1.0 - lam_init)
        for h in range(heads):
            rs = head_rows(h)
            blk = acc_ref[rs, :] / l_ref[rs, :]
            o_ref[0, :, head_cols(h)] = _rms(blk[0:n_tok] - lam * blk[n_tok:group], sub_gain)


def _attn_sample(q, k_new, v_new, cache_k, cache_v, layer, page_table, lam_vecs, subln, slopes, lam_init):
    n_req, n_tok, width = q.shape
    n_pages = page_table.shape[1]
    depth, n_pool, page, heads, _ = cache_k.shape
    group = 2 * n_tok
    assert group % BF16_ROWS == 0 and page % BF16_ROWS == 0 and n_tok <= page
    rows = heads * group
    pages = 8
    while n_pages % pages:
        pages //= 2
    chunk = pages * page
    past = n_pages * page
    cache_k = cache_k.reshape(depth * n_pool, page * heads, HEAD_WIDTH)
    cache_v = cache_v.reshape(depth * n_pool, page * heads, HEAD_WIDTH)

    def page_spec(p):
        return pl.BlockSpec(
            (1, page * heads, HEAD_WIDTH),
            lambda b, c, pt: (layer * n_pool + pt[b * n_pages + c * pages + p], 0, 0))

    tok_spec = pl.BlockSpec((1, n_tok, width), lambda b, c, pt: (b, 0, 0))
    grid_spec = pltpu.PrefetchScalarGridSpec(
        num_scalar_prefetch=1,
        grid=(n_req, n_pages // pages),
        in_specs=[
            pl.BlockSpec(memory_space=pltpu.SMEM),
            tok_spec, tok_spec, tok_spec,
            pl.BlockSpec((4, HEAD_DIM), lambda b, c, pt: (0, 0)),
            pl.BlockSpec((1, HEAD_WIDTH), lambda b, c, pt: (0, 0)),
        ] + [page_spec(p) for p in range(pages)] * 2,
        out_specs=tok_spec,
        scratch_shapes=[
            pltpu.VMEM((rows, HEAD_WIDTH), BF16),
            pltpu.VMEM((rows, chunk), F32),
            pltpu.VMEM((rows, 1), F32),
            pltpu.VMEM((rows, 1), F32),
            pltpu.VMEM((rows, 1), F32),
            pltpu.VMEM((rows, HEAD_WIDTH), F32),
            pltpu.VMEM((heads, chunk, HEAD_WIDTH), BF16),
            pltpu.VMEM((heads, chunk, HEAD_WIDTH), BF16),
        ],
    )
    return pl.pallas_call(
        functools.partial(_attn_sample_kernel, pages=pages, heads=heads, past=past, lam_init=lam_init),
        grid_spec=grid_spec,
        out_shape=jax.ShapeDtypeStruct((n_req, n_tok, width), F32),
        compiler_params=_params("parallel", "arbitrary"),
        name="attn_sample",
    )(page_table.reshape(-1), slopes, q, k_new, v_new, lam_vecs, subln.reshape(1, HEAD_WIDTH),
      *([cache_k] * pages), *([cache_v] * pages))


def _conv_kernel(*refs, taps, tc, rc, has_state):
    if has_state:
        x_ref, st_ref, w_ref, cb_ref, g_ref, b_ref, o_ref, s_ref, wb_ref = refs
    else:
        x_ref, w_ref, cb_ref, g_ref, b_ref, o_ref, s_ref, wb_ref = refs
    i = pl.program_id(1)
    halo = s_ref.shape[0] - tc
    lead = halo - (taps - 1)
    ch = s_ref.shape[1]
    t0 = pl.multiple_of(i * tc, SUBLANES)

    s_ref[halo:halo + tc, :] = x_ref[0, pl.ds(t0, tc), :]

    @pl.when(i == 0)
    def _():
        s_ref[0:halo, :] = jnp.zeros((halo, ch), F32)
        if has_state:
            s_ref[lead:halo, :] = st_ref[0]

    @pl.when(i > 0)
    def _():
        s_ref[0:halo, :] = x_ref[0, pl.ds(pl.multiple_of(t0 - halo, SUBLANES), halo), :]

    for j in range(taps):
        wb_ref[j] = jnp.broadcast_to(w_ref[j:j + 1, :], (SUBLANES, ch))

    def chunk(r, carry):
        r0 = pl.multiple_of(r * rc, SUBLANES)
        parts = []
        for lb in range(ch // LANES):
            ls = slice(lb * LANES, (lb + 1) * LANES)
            win = s_ref[pl.ds(r0, rc + halo), ls]
            shifted = [win] + [win[sh:sh + rc + halo - SUBLANES] for sh in range(1, SUBLANES)]
            acc = jnp.zeros((rc, LANES), F32)
            for j in range(taps):
                q, sh = divmod(lead + j, SUBLANES)
                w = wb_ref[j, :, ls]
                if rc > SUBLANES:
                    w = jnp.concatenate([w] * (rc // SUBLANES), axis=0)
                acc = acc + shifted[sh][q * SUBLANES:q * SUBLANES + rc] * w
            parts.append(acc)
        y = jnp.concatenate(parts, axis=1) + cb_ref[...]
        mu = jnp.mean(y, axis=-1, keepdims=True)
        yc = y - mu
        var = jnp.mean(yc * yc, axis=-1, keepdims=True)
        z = yc * lax.rsqrt(var + EPS) * g_ref[...] + b_ref[...]
        o_ref[0, pl.ds(r0, rc), :] = (z * jax.nn.sigmoid(z)).astype(o_ref.dtype)
        return carry

    lax.fori_loop(0, tc // rc, chunk, 0)


def _conv_group(glu, state, conv_w, conv_b, norm_g, norm_b):
    batch, seq, ch = glu.shape
    taps = conv_w.shape[0]
    halo = -(-(taps - 1) // SUBLANES) * SUBLANES
    tc = _seq_tile(seq, 704)
    if seq // tc > 1:
        assert tc >= halo
    rc = BF16_ROWS if tc % BF16_ROWS == 0 else SUBLANES
    has_state = state is not None
    vec = pl.BlockSpec((1, ch), lambda b, i: (0, 0))
    in_specs = [pl.BlockSpec((1, seq, ch), lambda b, i: (b, 0, 0))]
    args = [glu]
    if has_state:
        in_specs.append(pl.BlockSpec((1, taps - 1, ch), lambda b, i: (b, 0, 0)))
        args.append(state)
    in_specs += [pl.BlockSpec((taps, ch), lambda b, i: (0, 0)), vec, vec, vec]
    args += [conv_w, conv_b.reshape(1, ch), norm_g.reshape(1, ch), norm_b.reshape(1, ch)]
    return pl.pallas_call(
        functools.partial(_conv_kernel, taps=taps, tc=tc, rc=rc, has_state=has_state),
        grid=(batch, seq // tc),
        in_specs=in_specs,
        out_specs=pl.BlockSpec((1, tc, ch), lambda b, i: (b, i, 0)),
        out_shape=jax.ShapeDtypeStruct((batch, seq, ch), F32),
        scratch_shapes=[
            pltpu.VMEM((halo + tc, ch), F32),
            pltpu.VMEM((taps, SUBLANES, ch), F32),
        ],
        compiler_params=_params("parallel", "arbitrary"),
        name="conv_group",
    )(*args)


def _mix_out_kernel(x_ref, a_ref, c_ref, wa_ref, wc_ref, o_ref):
    y = jnp.dot(a_ref[...].astype(BF16), wa_ref[...], preferred_element_type=F32)
    y += jnp.dot(c_ref[...].astype(BF16), wc_ref[...], preferred_element_type=F32)
    o_ref[...] = x_ref[...] + y


def _mix_out(x, att, conv, w_out):
    rows, d = x.shape
    wa = att.shape[1]
    wc = conv.shape[1]
    assert wa == wc and w_out.shape[0] == wa + wc
    tm = _row_tile(rows, 704)
    return pl.pallas_call(
        _mix_out_kernel,
        grid=(rows // tm,),
        in_specs=[
            pl.BlockSpec((tm, d), lambda i: (i, 0)),
            pl.BlockSpec((tm, wa), lambda i: (i, 0)),
            pl.BlockSpec((tm, wc), lambda i: (i, 0)),
            pl.BlockSpec((wa, d), lambda i: (0, 0)),
            pl.BlockSpec((wc, d), lambda i: (1, 0)),
        ],
        out_specs=pl.BlockSpec((tm, d), lambda i: (i, 0)),
        out_shape=jax.ShapeDtypeStruct((rows, d), F32),
        compiler_params=_params("parallel"),
        name="mix_out",
    )(x, att, conv, w_out, w_out)


def kernel(x_prompt, x_sample, cache_k, cache_v, state_conv, page_table, meta_tokens, ffn1_norm, ffn1_w_gate, ffn1_w_up, ffn1_w_down, mix_norm, w_in, lambda_q1, lambda_k1, lambda_q2, lambda_k2, attn_subln, conv_w, conv_b, conv_norm_g, conv_norm_b, w_out, ffn2_norm, ffn2_w_gate, ffn2_w_up, ffn2_w_down, final_norm):
    depth = w_in.shape[0]
    batch, seq0, d = x_prompt.shape
    n_meta = meta_tokens.shape[0]
    seq = seq0 + n_meta
    n_req, n_tok, _ = x_sample.shape
    taps = conv_w.shape[1]
    width = w_out.shape[1] // 2
    heads = width // HEAD_WIDTH
    slopes = jnp.array([2.0 ** (-8.0 * (i + 1) / heads) for i in range(heads)], F32)

    meta = jnp.broadcast_to(meta_tokens.astype(x_prompt.dtype)[None], (batch, n_meta, d))
    xp = jnp.concatenate([meta, x_prompt], axis=1).reshape(batch * seq, d)
    xs = x_sample.reshape(n_req * n_tok, d)

    outs = [[] for _ in range(6)]
    for l in range(depth):
        lam_init = 0.8 - 0.6 * math.exp(-0.3 * l)
        lam_vecs = jnp.stack([lambda_q1[l], lambda_k1[l], lambda_q2[l], lambda_k2[l]])
        wg1, wu1, wd1 = (w[l].astype(BF16) for w in (ffn1_w_gate, ffn1_w_up, ffn1_w_down))
        wg2, wu2, wd2 = (w[l].astype(BF16) for w in (ffn2_w_gate, ffn2_w_up, ffn2_w_down))
        w_in_l = w_in[l].astype(BF16)
        w_out_l = w_out[l].astype(BF16)
        last = l == depth - 1

        xp = _ffn(xp, ffn1_norm[l], wg1, wu1, wd1)
        xs = _ffn(xs, ffn1_norm[l], wg1, wu1, wd1)

        qp, kp, vp, glup = _mix_in(xp, mix_norm[l], w_in_l, width)
        shape_p = (batch, seq, width)
        att_p = _attn_prompt(qp.reshape(shape_p), kp.reshape(shape_p), vp.reshape(shape_p),
                             lam_vecs, attn_subln[l], slopes, lam_init)
        glup = glup.reshape(shape_p)
        conv_p = _conv_group(glup, None, conv_w[l], conv_b[l], conv_norm_g[l], conv_norm_b[l])
        xp = _mix_out(xp, att_p.reshape(batch * seq, width), conv_p.reshape(batch * seq, width), w_out_l)
        outs[0].append(kp.reshape(batch, seq, heads, HEAD_WIDTH))
        outs[1].append(vp.reshape(batch, seq, heads, HEAD_WIDTH))
        outs[2].append(glup[:, seq - (taps - 1):])

        qs, ks, vs, glus = _mix_in(xs, mix_norm[l], w_in_l, width)
        shape_s = (n_req, n_tok, width)
        att_s = _attn_sample(qs.reshape(shape_s), ks.reshape(shape_s), vs.reshape(shape_s),
                             cache_k, cache_v, l, page_table, lam_vecs, attn_subln[l], slopes, lam_init)
        glus = glus.reshape(shape_s)
        conv_s = _conv_group(glus, state_conv[l], conv_w[l], conv_b[l], conv_norm_g[l], conv_norm_b[l])
        xs = _mix_out(xs, att_s.reshape(n_req * n_tok, width), conv_s.reshape(n_req * n_tok, width), w_out_l)
        outs[3].append(ks.reshape(n_req, n_tok, heads, HEAD_WIDTH))
        outs[4].append(vs.reshape(n_req, n_tok, heads, HEAD_WIDTH))
        outs[5].append(jnp.concatenate([state_conv[l], glus], axis=1)[:, -(taps - 1):])

        fg = final_norm if last else None
        xp = _ffn(xp, ffn2_norm[l], wg2, wu2, wd2, fg)
        xs = _ffn(xs, ffn2_norm[l], wg2, wu2, wd2, fg)

    y_prompt = xp.reshape(batch, seq, d)[:, n_meta:]
    y_sample = xs.reshape(n_req, n_tok, d)
    return (y_prompt, y_sample) + tuple(jnp.stack(o) for o in outs)
```

```python
import functools
import math

import jax
import jax.numpy as jnp
from jax import lax
from jax.experimental import pallas as pl
from jax.experimental.pallas import tpu as pltpu

EPS = 1e-6
HEAD_DIM = 64
HEAD_WIDTH = 2 * HEAD_DIM
LANES = 128
SUBLANES = 8
BF16_ROWS = 16
LOG2E = math.log2(math.e)
VMEM_LIMIT_BYTES = 56 * 1024 * 1024

BF16 = jnp.bfloat16
F32 = jnp.float32
_NT = (((1,), (1,)), ((), ()))


def _row_tile(rows, target):
    best = None
    for t in range(BF16_ROWS, min(rows, target) + 1, BF16_ROWS):
        if rows % t == 0:
            best = t
    assert best is not None, (rows, target)
    return best


def _col_tile(cols, target):
    best = None
    for t in range(LANES, min(cols, target) + 1, LANES):
        if cols % t == 0:
            best = t
    assert best is not None, (cols, target)
    return best


def _idiv(x, n):
    if n & (n - 1) == 0:
        return lax.shift_right_logical(x, n.bit_length() - 1)
    return x // n


def _imod(x, n):
    if n & (n - 1) == 0:
        return lax.bitwise_and(x, n - 1)
    return lax.rem(x, n)


def _rms(x, gain):
    return x * lax.rsqrt(jnp.mean(x * x, axis=-1, keepdims=True) + EPS) * gain


def _params(*semantics):
    return pltpu.CompilerParams(dimension_semantics=semantics, vmem_limit_bytes=VMEM_LIMIT_BYTES)


def _ffn_kernel(*refs, final):
    if final:
        x_ref, g_ref, wg_ref, wu_ref, wd_ref, fg_ref, o_ref, h_ref = refs
    else:
        x_ref, g_ref, wg_ref, wu_ref, wd_ref, o_ref, h_ref = refs
    j = pl.program_id(1)

    @pl.when(j == 0)
    def _():
        h_ref[...] = _rms(x_ref[...], g_ref[...]).astype(BF16)
        o_ref[...] = jnp.zeros_like(o_ref)

    h = h_ref[...]
    gate = jnp.dot(h, wg_ref[...], preferred_element_type=F32)
    up = jnp.dot(h, wu_ref[...], preferred_element_type=F32)
    act = (gate * jax.nn.sigmoid(gate) * up).astype(BF16)
    o_ref[...] += jnp.dot(act, wd_ref[...], preferred_element_type=F32)

    @pl.when(j == pl.num_programs(1) - 1)
    def _():
        y = x_ref[...] + 0.5 * o_ref[...]
        if final:
            y = _rms(y, fg_ref[...])
        o_ref[...] = y


def _ffn(x, gain, w_gate, w_up, w_down, final_gain=None):
    rows, d = x.shape
    f = w_gate.shape[1]
    tm = _row_tile(rows, 704)
    tf = _col_tile(f, 512)
    final = final_gain is not None
    in_specs = [
        pl.BlockSpec((tm, d), lambda i, j: (i, 0)),
        pl.BlockSpec((1, d), lambda i, j: (0, 0)),
        pl.BlockSpec((d, tf), lambda i, j: (0, j)),
        pl.BlockSpec((d, tf), lambda i, j: (0, j)),
        pl.BlockSpec((tf, d), lambda i, j: (j, 0)),
    ]
    args = [x, gain.reshape(1, d), w_gate, w_up, w_down]
    if final:
        in_specs.append(pl.BlockSpec((1, d), lambda i, j: (0, 0)))
        args.append(final_gain.reshape(1, d))
    return pl.pallas_call(
        functools.partial(_ffn_kernel, final=final),
        grid=(rows // tm, f // tf),
        in_specs=in_specs,
        out_specs=pl.BlockSpec((tm, d), lambda i, j: (i, 0)),
        out_shape=jax.ShapeDtypeStruct((rows, d), F32),
        scratch_shapes=[pltpu.VMEM((tm, d), BF16)],
        compiler_params=_params("parallel", "arbitrary"),
        name="half_ffn_final" if final else "half_ffn",
    )(*args)


def _mix_in_kernel(x_ref, g_ref, wa_ref, wg_ref, q_ref, k_ref, v_ref, glu_ref, h_ref):
    j = pl.program_id(1)

    @pl.when(j == 0)
    def _():
        h_ref[...] = _rms(x_ref[...], g_ref[...]).astype(BF16)

    h = h_ref[...]
    r = jnp.dot(h, wa_ref[...], preferred_element_type=F32)

    @pl.when(j == 0)
    def _():
        q_ref[...] = r * (HEAD_DIM ** -0.5)

    @pl.when(j == 1)
    def _():
        k_ref[...] = r

    @pl.when(j == 2)
    def _():
        v_ref[...] = r

    @pl.when(j == 3)
    def _():
        gate = jnp.dot(h, wg_ref[...], preferred_element_type=F32)
        glu_ref[...] = r * jax.nn.sigmoid(gate)


def _mix_in(x, gain, w_in, width):
    rows, d = x.shape
    assert w_in.shape[1] == 5 * width
    tm = _row_tile(rows, 704)
    out_spec = pl.BlockSpec((tm, width), lambda i, j: (i, 0))
    out_sds = jax.ShapeDtypeStruct((rows, width), F32)
    return pl.pallas_call(
        _mix_in_kernel,
        grid=(rows // tm, 4),
        in_specs=[
            pl.BlockSpec((tm, d), lambda i, j: (i, 0)),
            pl.BlockSpec((1, d), lambda i, j: (0, 0)),
            pl.BlockSpec((d, width), lambda i, j: (0, j)),
            pl.BlockSpec((d, width), lambda i, j: (0, 4)),
        ],
        out_specs=[out_spec] * 4,
        out_shape=[out_sds] * 4,
        scratch_shapes=[pltpu.VMEM((tm, d), BF16)],
        compiler_params=_params("parallel", "arbitrary"),
        name="mix_in",
    )(x, gain.reshape(1, d), w_in, w_in)


def _diff_lambda(lam_ref, lam_init):
    lv = lam_ref[...]
    d1 = jnp.sum(lv[0:1] * lv[1:2], axis=-1, keepdims=True)
    d2 = jnp.sum(lv[2:3] * lv[3:4], axis=-1, keepdims=True)
    return jnp.exp(d1) - jnp.exp(d2) + lam_init


def _attn_prompt_kernel(slopes_ref, q_ref, k_ref, v_ref, lam_ref, sub_ref, o_ref,
                        k1_ref, k2_ref, vb_ref, dist_ref, sd_ref, m_ref, l_ref, acc_ref,
                        *, seq, tq, tk, lam_init):
    head = pl.program_id(1)
    slope2 = slopes_ref[head] * LOG2E
    seq_pad = vb_ref.shape[0]
    k_refs = (k1_ref, k2_ref)

    k = k_ref[0]
    k1_ref[0:seq, :] = k[:, :HEAD_DIM].astype(BF16)
    k2_ref[0:seq, :] = k[:, HEAD_DIM:].astype(BF16)
    vb_ref[0:seq, :] = v_ref[0].astype(BF16)
    if seq_pad > seq:
        k1_ref[seq:seq_pad, :] = jnp.zeros((seq_pad - seq, HEAD_DIM), BF16)
        k2_ref[seq:seq_pad, :] = jnp.zeros((seq_pad - seq, HEAD_DIM), BF16)
        vb_ref[seq:seq_pad, :] = jnp.zeros((seq_pad - seq, HEAD_WIDTH), BF16)
    dist = (lax.broadcasted_iota(jnp.int32, (tq, tk), 1)
            - lax.broadcasted_iota(jnp.int32, (tq, tk), 0))
    dist_ref[...] = dist
    sd_ref[...] = slope2 * dist.astype(F32)
    lam = _diff_lambda(lam_ref, lam_init)
    sub_gain = sub_ref[...] * (1.0 - lam_init)
    stat = (tq, LANES)

    def q_tile(i, carry):
        q0 = pl.multiple_of(i * tq, SUBLANES)
        qt = q_ref[0, pl.ds(q0, tq), :] * LOG2E
        qs = (qt[:, :HEAD_DIM].astype(BF16), qt[:, HEAD_DIM:].astype(BF16))
        m_ref[...] = jnp.full(m_ref.shape, -jnp.inf, F32)
        l_ref[...] = jnp.zeros_like(l_ref)
        acc_ref[...] = jnp.zeros_like(acc_ref)

        def kv_tile(j, carry, masked):
            k0 = pl.multiple_of(j * tk, tk)
            vt = vb_ref[pl.ds(k0, tk), :]
            off = q0 - k0
            tile_bias = -slope2 * off.astype(F32)
            for mp in range(2):
                kmp = k_refs[mp][pl.ds(k0, tk), :]
                s = lax.dot_general(qs[mp], kmp, _NT, preferred_element_type=F32) + sd_ref[...]
                if masked:
                    s = jnp.where(dist_ref[...] <= off, s, -jnp.inf)
                m_old = m_ref[mp]
                m_tile = jnp.broadcast_to(jnp.max(s, axis=-1, keepdims=True), stat) + tile_bias
                m_new = jnp.maximum(m_old, m_tile)
                corr = jnp.exp2(m_old - m_new)
                shift = tile_bias - m_new
                p = jnp.exp2(s + jnp.concatenate([shift] * (tk // LANES), axis=1))
                l_ref[mp] = l_ref[mp] * corr + jnp.broadcast_to(
                    jnp.sum(p, axis=-1, keepdims=True), stat)
                acc_ref[mp] = acc_ref[mp] * corr + jnp.dot(
                    p.astype(BF16), vt, preferred_element_type=F32)
                m_ref[mp] = m_new
            return carry

        n_full = q0 // tk
        n_all = (q0 + tq + tk - 1) // tk
        lax.fori_loop(0, n_full, functools.partial(kv_tile, masked=False), 0)
        lax.fori_loop(n_full, n_all, functools.partial(kv_tile, masked=True), 0)

        o1 = acc_ref[0] / l_ref[0]
        o2 = acc_ref[1] / l_ref[1]
        o_ref[0, pl.ds(q0, tq), :] = _rms(o1 - lam * o2, sub_gain)
        return carry

    lax.fori_loop(0, seq // tq, q_tile, 0)


def _seq_tile(seq, target):
    best = None
    for t in range(SUBLANES, min(seq, target) + 1, SUBLANES):
        if seq % t == 0:
            best = t
    assert best is not None, (seq, target)
    return best


def _attn_prompt(q, k, v, lam_vecs, subln, slopes, lam_init):
    batch, seq, width = q.shape
    heads = width // HEAD_WIDTH
    assert HEAD_WIDTH == LANES
    tq = _seq_tile(seq, 384)
    tk = 512
    seq_pad = -(-seq // tk) * tk
    blk = pl.BlockSpec((1, seq, HEAD_WIDTH), lambda b, h: (b, 0, h))
    return pl.pallas_call(
        functools.partial(_attn_prompt_kernel, seq=seq, tq=tq, tk=tk, lam_init=lam_init),
        grid=(batch, heads),
        in_specs=[
            pl.BlockSpec(memory_space=pltpu.SMEM),
            blk, blk, blk,
            pl.BlockSpec((4, HEAD_DIM), lambda b, h: (0, 0)),
            pl.BlockSpec((1, HEAD_WIDTH), lambda b, h: (0, 0)),
        ],
        out_specs=blk,
        out_shape=jax.ShapeDtypeStruct((batch, seq, width), F32),
        scratch_shapes=[
            pltpu.VMEM((seq_pad, HEAD_DIM), BF16),
            pltpu.VMEM((seq_pad, HEAD_DIM), BF16),
            pltpu.VMEM((seq_pad, HEAD_WIDTH), BF16),
            pltpu.VMEM((tq, tk), jnp.int32),
            pltpu.VMEM((tq, tk), F32),
            pltpu.VMEM((2, tq, LANES), F32),
            pltpu.VMEM((2, tq, LANES), F32),
            pltpu.VMEM((2, tq, HEAD_WIDTH), F32),
        ],
        compiler_params=_params("parallel", "parallel"),
        name="attn_prompt",
    )(slopes, q, k, v, lam_vecs, subln.reshape(1, HEAD_WIDTH))


def _attn_sample_kernel(pt_ref, slopes_ref, q_ref, kn_ref, vn_ref, lam_ref, sub_ref, *rest,
                        pages, heads, past, lam_init):
    k_refs = rest[:pages]
    v_refs = rest[pages:2 * pages]
    o_ref, wq_ref, b0_ref, sl_ref, m_ref, l_ref, acc_ref, kc_ref, vc_ref = rest[2 * pages:]
    del pt_ref
    c = pl.program_id(1)
    n_tok = q_ref.shape[1]
    width, cols = wq_ref.shape
    group = 2 * n_tok
    page = k_refs[0].shape[1] // heads
    chunk = pages * page

    def head_group(h):
        return slice(h * group, (h + 1) * group)

    def head_cols(h):
        return slice(h * HEAD_WIDTH, (h + 1) * HEAD_WIDTH)

    def as_rows(stat):
        return jnp.broadcast_to(stat, (HEAD_WIDTH, cols)).T

    @pl.when(c == 0)
    def _():
        rep = jnp.concatenate([q_ref[0]] * (cols // n_tok), axis=0)
        r_io = lax.broadcasted_iota(jnp.int32, (cols, width), 0)
        c_io = lax.broadcasted_iota(jnp.int32, (cols, width), 1)
        keep = _idiv(c_io, HEAD_DIM) == _idiv(r_io, n_tok)
        wq_ref[...] = jnp.where(keep, rep, 0.0).T.astype(BF16)
        col = lax.broadcasted_iota(jnp.int32, (1, cols), 1)
        slope_row = jnp.zeros((1, cols), F32)
        for h in range(heads):
            slope_row = jnp.where(_idiv(col, group) == h, slopes_ref[h], slope_row)
        sl_ref[...] = slope_row
        key = lax.broadcasted_iota(jnp.int32, (chunk, cols), 0)
        tok = _imod(lax.broadcasted_iota(jnp.int32, (chunk, cols), 1), n_tok)
        b0_ref[...] = slope_row * (key - tok).astype(F32)
        m_ref[...] = jnp.full(m_ref.shape, -jnp.inf, F32)
        l_ref[...] = jnp.zeros_like(l_ref)
        acc_ref[...] = jnp.zeros_like(acc_ref)

    def update(s, n_keys):
        m_old = m_ref[...]
        m_new = jnp.maximum(m_old, jnp.max(s, axis=0, keepdims=True))
        corr = jnp.exp(m_old - m_new)
        p = jnp.exp(s - m_new)
        l_ref[...] = l_ref[...] * corr + jnp.sum(p, axis=0, keepdims=True)
        pb = p.T.astype(BF16)
        corr_rows = as_rows(corr)
        span = 2 if heads % 2 == 0 else 1
        for h0 in range(0, heads, span):
            z = jnp.dot(pb[h0 * group:(h0 + span) * group],
                        vc_ref[0:n_keys, h0 * HEAD_WIDTH:(h0 + span) * HEAD_WIDTH],
                        preferred_element_type=F32)
            for i in range(span):
                hg = head_group(h0 + i)
                acc_ref[hg, :] = acc_ref[hg, :] * corr_rows[hg] + z[i * group:(i + 1) * group,
                                                                    i * HEAD_WIDTH:(i + 1) * HEAD_WIDTH]
        m_ref[...] = m_new

    def scores(n_keys):
        return jnp.dot(kc_ref[0:n_keys, :], wq_ref[...], preferred_element_type=F32) + b0_ref[0:n_keys, :]

    for p in range(pages):
        for t in range(0, page, BF16_ROWS):
            for h in range(heads):
                src = pl.ds(t * heads + h, BF16_ROWS, stride=heads)
                dst = slice(p * page + t, p * page + t + BF16_ROWS)
                kc_ref[dst, head_cols(h)] = k_refs[p][0, src, :].astype(BF16)
                vc_ref[dst, head_cols(h)] = v_refs[p][0, src, :].astype(BF16)
    chunk_shift = sl_ref[...] * (c * chunk - past).astype(F32)
    update(scores(chunk) + chunk_shift, chunk)

    @pl.when(c == pl.num_programs(1) - 1)
    def _():
        pad = jnp.zeros((page - n_tok, width), F32)
        kc_ref[0:page, :] = jnp.concatenate([kn_ref[0], pad], axis=0).astype(BF16)
        vc_ref[0:page, :] = jnp.concatenate([vn_ref[0], pad], axis=0).astype(BF16)
        key = lax.broadcasted_iota(jnp.int32, (page, cols), 0)
        tok = _imod(lax.broadcasted_iota(jnp.int32, (page, cols), 1), n_tok)
        update(jnp.where(key <= tok, scores(page), -jnp.inf), page)

        lam = _diff_lambda(lam_ref, lam_init)
        sub_gain = sub_ref[...] * (1.0 - lam_init)
        l_rows = as_rows(l_ref[...])
        for h in range(heads):
            hg = head_group(h)
            blk = acc_ref[hg, :] / l_rows[hg]
            o_ref[0, :, head_cols(h)] = _rms(blk[0:n_tok] - lam * blk[n_tok:group], sub_gain)


def _attn_sample(q, k_new, v_new, cache_k, cache_v, layer, page_table, lam_vecs, subln, slopes, lam_init):
    n_req, n_tok, width = q.shape
    n_pages = page_table.shape[1]
    depth, n_pool, page, heads, _ = cache_k.shape
    group = 2 * n_tok
    assert group % BF16_ROWS == 0 and page % BF16_ROWS == 0 and n_tok <= page
    cols = heads * group
    pages = 8
    while n_pages % pages:
        pages //= 2
    chunk = pages * page
    past = n_pages * page
    cache_k = cache_k.reshape(depth * n_pool, page * heads, HEAD_WIDTH)
    cache_v = cache_v.reshape(depth * n_pool, page * heads, HEAD_WIDTH)

    def page_spec(p):
        return pl.BlockSpec(
            (1, page * heads, HEAD_WIDTH),
            lambda b, c, pt: (layer * n_pool + pt[b * n_pages + c * pages + p], 0, 0))

    tok_spec = pl.BlockSpec((1, n_tok, width), lambda b, c, pt: (b, 0, 0))
    grid_spec = pltpu.PrefetchScalarGridSpec(
        num_scalar_prefetch=1,
        grid=(n_req, n_pages // pages),
        in_specs=[
            pl.BlockSpec(memory_space=pltpu.SMEM),
            tok_spec, tok_spec, tok_spec,
            pl.BlockSpec((4, HEAD_DIM), lambda b, c, pt: (0, 0)),
            pl.BlockSpec((1, HEAD_WIDTH), lambda b, c, pt: (0, 0)),
        ] + [page_spec(p) for p in range(pages)] * 2,
        out_specs=tok_spec,
        scratch_shapes=[
            pltpu.VMEM((width, cols), BF16),
            pltpu.VMEM((chunk, cols), F32),
            pltpu.VMEM((1, cols), F32),
            pltpu.VMEM((1, cols), F32),
            pltpu.VMEM((1, cols), F32),
            pltpu.VMEM((cols, HEAD_WIDTH), F32),
            pltpu.VMEM((chunk, width), BF16),
            pltpu.VMEM((chunk, width), BF16),
        ],
    )
    return pl.pallas_call(
        functools.partial(_attn_sample_kernel, pages=pages, heads=heads, past=past, lam_init=lam_init),
        grid_spec=grid_spec,
        out_shape=jax.ShapeDtypeStruct((n_req, n_tok, width), F32),
        compiler_params=_params("parallel", "arbitrary"),
        name="attn_sample",
    )(page_table.reshape(-1), slopes, q, k_new, v_new, lam_vecs, subln.reshape(1, HEAD_WIDTH),
      *([cache_k] * pages), *([cache_v] * pages))


def _conv_kernel(*refs, taps, rc, halo, has_hist):
    if has_hist:
        x_ref, h_ref, w_ref, cb_ref, g_ref, b_ref, o_ref, wb_ref = refs
    else:
        x_ref, w_ref, cb_ref, g_ref, b_ref, o_ref, wb_ref = refs
    seq, ch = x_ref.shape[1], x_ref.shape[2]
    lead = halo - (taps - 1)
    n_out = rc // SUBLANES
    n_win = n_out + halo // SUBLANES
    n_lb = ch // LANES

    for j in range(taps):
        wb_ref[j] = jnp.broadcast_to(w_ref[j:j + 1, :], (SUBLANES, ch))
    sublane = lax.broadcasted_iota(jnp.int32, (SUBLANES, LANES), 0)

    def conv_rows(load_tile, r0):
        parts = [[None] * n_lb for _ in range(n_out)]
        for lb in range(n_lb):
            ls = slice(lb * LANES, (lb + 1) * LANES)
            xv = [load_tile(i, ls) for i in range(n_win)]
            out = [None] * n_out
            for s in range(SUBLANES):
                a_vals = [a for a in range(n_win) if lead <= SUBLANES * a + s <= lead + taps - 1]
                if not a_vals:
                    continue
                part = [None] * (n_out + (1 if s else 0))
                for a in a_vals:
                    w = wb_ref[SUBLANES * a + s - lead, :, ls]
                    for m in range(len(part)):
                        term = xv[m + a] * w
                        part[m] = term if part[m] is None else part[m] + term
                for m in range(n_out):
                    if s == 0:
                        u = part[m]
                    else:
                        u = pltpu.roll(jnp.where(sublane >= s, part[m], part[m + 1]), SUBLANES - s, 0)
                    out[m] = u if out[m] is None else out[m] + u
            for m in range(n_out):
                parts[m][lb] = out[m]
        rows = []
        for m in range(n_out):
            y = jnp.concatenate(parts[m], axis=1) + cb_ref[...]
            mu = jnp.mean(y, axis=-1, keepdims=True)
            yc = y - mu
            var = jnp.mean(yc * yc, axis=-1, keepdims=True)
            z = yc * lax.rsqrt(var + EPS) * g_ref[...] + b_ref[...]
            rows.append(z * jax.nn.sigmoid(z))
        o_ref[0, pl.ds(r0, rc), :] = jnp.concatenate(rows, axis=0).astype(o_ref.dtype)

    n_static = min(-(-halo // rc), seq // rc)
    for r in range(n_static):
        def static_tile(i, ls, r=r):
            row = r * rc - halo + i * SUBLANES
            if row >= 0:
                return x_ref[0, row:row + SUBLANES, ls]
            if has_hist:
                return h_ref[0, halo + row:halo + row + SUBLANES, ls]
            return jnp.zeros((SUBLANES, LANES), F32)

        conv_rows(static_tile, r * rc)

    def chunk(r, carry):
        r0 = pl.multiple_of(r * rc, SUBLANES)

        def tile(i, ls):
            return x_ref[0, pl.ds(pl.multiple_of(r0 - halo + i * SUBLANES, SUBLANES), SUBLANES), ls]

        conv_rows(tile, r0)
        return carry

    if seq // rc > n_static:
        lax.fori_loop(n_static, seq // rc, chunk, 0)


def _conv_group(glu, state, conv_w, conv_b, norm_g, norm_b):
    batch, seq, ch = glu.shape
    taps = conv_w.shape[0]
    halo = -(-(taps - 1) // SUBLANES) * SUBLANES
    rc = _seq_tile(seq, 48)
    has_hist = state is not None
    out_dtype = BF16 if rc % BF16_ROWS == 0 else F32
    vec = pl.BlockSpec((1, ch), lambda b: (0, 0))
    in_specs = [pl.BlockSpec((1, seq, ch), lambda b: (b, 0, 0))]
    args = [glu]
    if has_hist:
        in_specs.append(pl.BlockSpec((1, halo, ch), lambda b: (b, 0, 0)))
        args.append(jnp.pad(state, ((0, 0), (halo - (taps - 1), 0), (0, 0))))
    in_specs += [pl.BlockSpec((taps, ch), lambda b: (0, 0)), vec, vec, vec]
    args += [conv_w, conv_b.reshape(1, ch), norm_g.reshape(1, ch), norm_b.reshape(1, ch)]
    return pl.pallas_call(
        functools.partial(_conv_kernel, taps=taps, rc=rc, halo=halo, has_hist=has_hist),
        grid=(batch,),
        in_specs=in_specs,
        out_specs=pl.BlockSpec((1, seq, ch), lambda b: (b, 0, 0)),
        out_shape=jax.ShapeDtypeStruct((batch, seq, ch), out_dtype),
        scratch_shapes=[pltpu.VMEM((taps, SUBLANES, ch), F32)],
        compiler_params=_params("parallel"),
        name="conv_group",
    )(*args)


def _mix_out_kernel(x_ref, a_ref, c_ref, wa_ref, wc_ref, o_ref):
    y = jnp.dot(a_ref[...].astype(BF16), wa_ref[...], preferred_element_type=F32)
    y += jnp.dot(c_ref[...].astype(BF16), wc_ref[...], preferred_element_type=F32)
    o_ref[...] = x_ref[...] + y


def _mix_out(x, att, conv, w_out):
    rows, d = x.shape
    wa = att.shape[1]
    wc = conv.shape[1]
    assert wa == wc and w_out.shape[0] == wa + wc
    tm = _row_tile(rows, 704)
    return pl.pallas_call(
        _mix_out_kernel,
        grid=(rows // tm,),
        in_specs=[
            pl.BlockSpec((tm, d), lambda i: (i, 0)),
            pl.BlockSpec((tm, wa), lambda i: (i, 0)),
            pl.BlockSpec((tm, wc), lambda i: (i, 0)),
            pl.BlockSpec((wa, d), lambda i: (0, 0)),
            pl.BlockSpec((wc, d), lambda i: (1, 0)),
        ],
        out_specs=pl.BlockSpec((tm, d), lambda i: (i, 0)),
        out_shape=jax.ShapeDtypeStruct((rows, d), F32),
        compiler_params=_params("parallel"),
        name="mix_out",
    )(x, att, conv, w_out, w_out)


def kernel(x_prompt, x_sample, cache_k, cache_v, state_conv, page_table, meta_tokens, ffn1_norm, ffn1_w_gate, ffn1_w_up, ffn1_w_down, mix_norm, w_in, lambda_q1, lambda_k1, lambda_q2, lambda_k2, attn_subln, conv_w, conv_b, conv_norm_g, conv_norm_b, w_out, ffn2_norm, ffn2_w_gate, ffn2_w_up, ffn2_w_down, final_norm):
    depth = w_in.shape[0]
    batch, seq0, d = x_prompt.shape
    n_meta = meta_tokens.shape[0]
    seq = seq0 + n_meta
    n_req, n_tok, _ = x_sample.shape
    taps = conv_w.shape[1]
    width = w_out.shape[1] // 2
    heads = width // HEAD_WIDTH
    slopes = jnp.array([2.0 ** (-8.0 * (i + 1) / heads) for i in range(heads)], F32)

    meta = jnp.broadcast_to(meta_tokens.astype(x_prompt.dtype)[None], (batch, n_meta, d))
    xp = jnp.concatenate([meta, x_prompt], axis=1).reshape(batch * seq, d)
    xs = x_sample.reshape(n_req * n_tok, d)

    outs = [[] for _ in range(6)]
    for l in range(depth):
        lam_init = 0.8 - 0.6 * math.exp(-0.3 * l)
        lam_vecs = jnp.stack([lambda_q1[l], lambda_k1[l], lambda_q2[l], lambda_k2[l]])
        wg1, wu1, wd1 = (w[l].astype(BF16) for w in (ffn1_w_gate, ffn1_w_up, ffn1_w_down))
        wg2, wu2, wd2 = (w[l].astype(BF16) for w in (ffn2_w_gate, ffn2_w_up, ffn2_w_down))
        w_in_l = w_in[l].astype(BF16)
        w_out_l = w_out[l].astype(BF16)
        last = l == depth - 1

        xp = _ffn(xp, ffn1_norm[l], wg1, wu1, wd1)
        xs = _ffn(xs, ffn1_norm[l], wg1, wu1, wd1)

        qp, kp, vp, glup = _mix_in(xp, mix_norm[l], w_in_l, width)
        shape_p = (batch, seq, width)
        att_p = _attn_prompt(qp.reshape(shape_p), kp.reshape(shape_p), vp.reshape(shape_p),
                             lam_vecs, attn_subln[l], slopes, lam_init)
        glup = glup.reshape(shape_p)
        conv_p = _conv_group(glup, None, conv_w[l], conv_b[l], conv_norm_g[l], conv_norm_b[l])
        xp = _mix_out(xp, att_p.reshape(batch * seq, width), conv_p.reshape(batch * seq, width), w_out_l)
        outs[0].append(kp.reshape(batch, seq, heads, HEAD_WIDTH))
        outs[1].append(vp.reshape(batch, seq, heads, HEAD_WIDTH))
        outs[2].append(glup[:, seq - (taps - 1):])

        qs, ks, vs, glus = _mix_in(xs, mix_norm[l], w_in_l, width)
        shape_s = (n_req, n_tok, width)
        att_s = _attn_sample(qs.reshape(shape_s), ks.reshape(shape_s), vs.reshape(shape_s),
                             cache_k, cache_v, l, page_table, lam_vecs, attn_subln[l], slopes, lam_init)
        glus = glus.reshape(shape_s)
        conv_s = _conv_group(glus, state_conv[l], conv_w[l], conv_b[l], conv_norm_g[l], conv_norm_b[l])
        xs = _mix_out(xs, att_s.reshape(n_req * n_tok, width), conv_s.reshape(n_req * n_tok, width), w_out_l)
        outs[3].append(ks.reshape(n_req, n_tok, heads, HEAD_WIDTH))
        outs[4].append(vs.reshape(n_req, n_tok, heads, HEAD_WIDTH))
        outs[5].append(jnp.concatenate([state_conv[l], glus], axis=1)[:, -(taps - 1):])

        fg = final_norm if last else None
        xp = _ffn(xp, ffn2_norm[l], wg2, wu2, wd2, fg)
        xs = _ffn(xs, ffn2_norm[l], wg2, wu2, wd2, fg)

    y_prompt = xp.reshape(batch, seq, d)[:, n_meta:]
    y_sample = xs.reshape(n_req, n_tok, d)
    return (y_prompt, y_sample) + tuple(jnp.stack(o) for o in outs)
```

```python
import functools
import math
from typing import NamedTuple

import jax
import jax.numpy as jnp
from jax import lax
from jax.experimental import pallas as pl
from jax.experimental.pallas import tpu as pltpu

EPS = 1e-6
HEAD_DIM = 64
HEAD_WIDTH = 2 * HEAD_DIM
LANES = 128
SUBLANES = 8
BF16_ROWS = 16
LOG2E = math.log2(math.e)
VMEM_LIMIT_BYTES = 56 * 1024 * 1024

BF16 = jnp.bfloat16
F32 = jnp.float32
_NT = (((1,), (1,)), ((), ()))


def _row_tile(rows, target):
    best = None
    for t in range(BF16_ROWS, min(rows, target) + 1, BF16_ROWS):
        if rows % t == 0:
            best = t
    assert best is not None, (rows, target)
    return best


def _col_tile(cols, target):
    best = None
    for t in range(LANES, min(cols, target) + 1, LANES):
        if cols % t == 0:
            best = t
    assert best is not None, (cols, target)
    return best


def _idiv(x, n):
    if n & (n - 1) == 0:
        return lax.shift_right_logical(x, n.bit_length() - 1)
    return x // n


def _imod(x, n):
    if n & (n - 1) == 0:
        return lax.bitwise_and(x, n - 1)
    return lax.rem(x, n)


def _rms(x, gain):
    return x * lax.rsqrt(jnp.mean(x * x, axis=-1, keepdims=True) + EPS) * gain


def _params(*semantics):
    return pltpu.CompilerParams(dimension_semantics=semantics, vmem_limit_bytes=VMEM_LIMIT_BYTES)


def _ffn_kernel(*refs, final, decode):
    if decode is not None:
        refs = refs[1:]
    x_ref, g_ref, wg_ref, wu_ref, wd_ref = refs[:5]
    pos = 5
    if final:
        fg_ref = refs[pos]
        pos += 1
    if decode is not None:
        dec_in = refs[pos:pos + decode.n_inputs]
        pos += decode.n_inputs
    o_ref = refs[pos]
    pos += 1
    if decode is not None:
        att_ref = refs[pos]
        pos += 1
    h_ref = refs[pos]
    dec_scratch = refs[pos + 1:]
    j = pl.program_id(1)

    @pl.when(j == 0)
    def _():
        h_ref[...] = _rms(x_ref[...], g_ref[...]).astype(BF16)
        o_ref[...] = jnp.zeros_like(o_ref)

    if decode is not None:
        step = pl.program_id(0) * pl.num_programs(1) + j
        c = lax.rem(step, decode.n_chunks)
        dec_start, dec_scores, dec_values, dec_finish = _decode_phases(c, decode, dec_in, att_ref, dec_scratch)
        pl.when(c == 0)(dec_start)

    h = h_ref[...]
    gate = jnp.dot(h, wg_ref[...], preferred_element_type=F32)
    if decode is not None:
        dec_state = dec_scores()
    up = jnp.dot(h, wu_ref[...], preferred_element_type=F32)
    if decode is not None:
        dec_values(*dec_state)
    act = (gate * jax.nn.sigmoid(gate) * up).astype(BF16)
    o_ref[...] += jnp.dot(act, wd_ref[...], preferred_element_type=F32)

    if decode is not None:
        pl.when(c == decode.n_chunks - 1)(dec_finish)

    @pl.when(j == pl.num_programs(1) - 1)
    def _():
        y = x_ref[...] + 0.5 * o_ref[...]
        if final:
            y = _rms(y, fg_ref[...])
        o_ref[...] = y


def _ffn(x, gain, w_gate, w_up, w_down, final_gain=None, decode_args=None):
    rows, d = x.shape
    f = w_gate.shape[1]
    tm = _row_tile(rows, 704)
    tf = _col_tile(f, 512 if decode_args is None else 256)
    grid = (rows // tm, f // tf)
    final = final_gain is not None
    in_specs = [
        pl.BlockSpec((tm, d), lambda i, j, *_: (i, 0)),
        pl.BlockSpec((1, d), lambda i, j, *_: (0, 0)),
        pl.BlockSpec((d, tf), lambda i, j, *_: (0, j)),
        pl.BlockSpec((d, tf), lambda i, j, *_: (0, j)),
        pl.BlockSpec((tf, d), lambda i, j, *_: (j, 0)),
    ]
    args = [x, gain.reshape(1, d), w_gate, w_up, w_down]
    if final:
        in_specs.append(pl.BlockSpec((1, d), lambda i, j, *_: (0, 0)))
        args.append(final_gain.reshape(1, d))
    out_specs = pl.BlockSpec((tm, d), lambda i, j, *_: (i, 0))
    out_shape = jax.ShapeDtypeStruct((rows, d), F32)
    scratch = [pltpu.VMEM((tm, d), BF16)]
    name = "half_ffn_final" if final else "half_ffn"
    if decode_args is None:
        return pl.pallas_call(
            functools.partial(_ffn_kernel, final=final, decode=None),
            grid=grid, in_specs=in_specs, out_specs=out_specs, out_shape=out_shape,
            scratch_shapes=scratch, compiler_params=_params("parallel", "arbitrary"), name=name,
        )(*args)

    decode, page_table, dec_specs, dec_args, att_spec, att_shape, dec_scratch = _decode_plan(
        grid, **decode_args)
    in_specs[0] = pl.BlockSpec((tm, d), lambda i, j, *_: (i, 0), pipeline_mode=pl.Buffered(1))
    grid_spec = pltpu.PrefetchScalarGridSpec(
        num_scalar_prefetch=1, grid=grid,
        in_specs=in_specs + dec_specs,
        out_specs=[out_specs, att_spec],
        scratch_shapes=scratch + dec_scratch,
    )
    y, att = pl.pallas_call(
        functools.partial(_ffn_kernel, final=final, decode=decode),
        grid_spec=grid_spec,
        out_shape=[out_shape, att_shape],
        compiler_params=_params("arbitrary", "arbitrary"),
        name=name + "_decode",
    )(page_table, *args, *dec_args)
    return y, att[:decode.n_req]


def _mix_in_kernel(x_ref, g_ref, wa_ref, wg_ref, q_ref, k_ref, v_ref, glu_ref, h_ref):
    j = pl.program_id(1)

    @pl.when(j == 0)
    def _():
        h_ref[...] = _rms(x_ref[...], g_ref[...]).astype(BF16)

    h = h_ref[...]
    r = jnp.dot(h, wa_ref[...], preferred_element_type=F32)

    @pl.when(j == 0)
    def _():
        q_ref[...] = r * (HEAD_DIM ** -0.5)

    @pl.when(j == 1)
    def _():
        k_ref[...] = r

    @pl.when(j == 2)
    def _():
        v_ref[...] = r

    @pl.when(j == 3)
    def _():
        gate = jnp.dot(h, wg_ref[...], preferred_element_type=F32)
        glu_ref[...] = r * jax.nn.sigmoid(gate)


def _mix_in(x, gain, w_in, width):
    rows, d = x.shape
    assert w_in.shape[1] == 5 * width
    tm = _row_tile(rows, 704)
    out_spec = pl.BlockSpec((tm, width), lambda i, j: (i, 0))
    out_sds = jax.ShapeDtypeStruct((rows, width), F32)
    return pl.pallas_call(
        _mix_in_kernel,
        grid=(rows // tm, 4),
        in_specs=[
            pl.BlockSpec((tm, d), lambda i, j: (i, 0)),
            pl.BlockSpec((1, d), lambda i, j: (0, 0)),
            pl.BlockSpec((d, width), lambda i, j: (0, j)),
            pl.BlockSpec((d, width), lambda i, j: (0, 4)),
        ],
        out_specs=[out_spec] * 4,
        out_shape=[out_sds] * 4,
        scratch_shapes=[pltpu.VMEM((tm, d), BF16)],
        compiler_params=_params("parallel", "arbitrary"),
        name="mix_in",
    )(x, gain.reshape(1, d), w_in, w_in)


def _diff_lambda(lam_ref, lam_init):
    lv = lam_ref[...]
    d1 = jnp.sum(lv[0:1] * lv[1:2], axis=-1, keepdims=True)
    d2 = jnp.sum(lv[2:3] * lv[3:4], axis=-1, keepdims=True)
    return jnp.exp(d1) - jnp.exp(d2) + lam_init


def _attn_prompt_kernel(slopes_ref, q_ref, k_ref, v_ref, lam_ref, sub_ref, o_ref,
                        k1_ref, k2_ref, vb_ref, dist_ref, sd_ref, m_ref, l_ref, acc_ref,
                        *, seq, tq, tk, lam_init):
    head = pl.program_id(1)
    slope2 = slopes_ref[head] * LOG2E
    seq_pad = vb_ref.shape[0]
    k_refs = (k1_ref, k2_ref)

    k = k_ref[0]
    k1_ref[0:seq, :] = k[:, :HEAD_DIM].astype(BF16)
    k2_ref[0:seq, :] = k[:, HEAD_DIM:].astype(BF16)
    vb_ref[0:seq, :] = v_ref[0].astype(BF16)
    if seq_pad > seq:
        k1_ref[seq:seq_pad, :] = jnp.zeros((seq_pad - seq, HEAD_DIM), BF16)
        k2_ref[seq:seq_pad, :] = jnp.zeros((seq_pad - seq, HEAD_DIM), BF16)
        vb_ref[seq:seq_pad, :] = jnp.zeros((seq_pad - seq, HEAD_WIDTH), BF16)
    dist = (lax.broadcasted_iota(jnp.int32, (tq, tk), 1)
            - lax.broadcasted_iota(jnp.int32, (tq, tk), 0))
    dist_ref[...] = dist
    sd_ref[...] = slope2 * dist.astype(F32)
    lam = _diff_lambda(lam_ref, lam_init)
    sub_gain = sub_ref[...] * (1.0 - lam_init)
    stat = (tq, LANES)

    def q_tile(i, carry):
        q0 = pl.multiple_of(i * tq, SUBLANES)
        qt = q_ref[0, pl.ds(q0, tq), :] * LOG2E
        qs = (qt[:, :HEAD_DIM].astype(BF16), qt[:, HEAD_DIM:].astype(BF16))
        m_ref[...] = jnp.full(m_ref.shape, -jnp.inf, F32)
        l_ref[...] = jnp.zeros_like(l_ref)
        acc_ref[...] = jnp.zeros_like(acc_ref)

        def kv_tile(j, carry, masked):
            k0 = pl.multiple_of(j * tk, tk)
            vt = vb_ref[pl.ds(k0, tk), :]
            off = q0 - k0
            tile_bias = -slope2 * off.astype(F32)
            for mp in range(2):
                kmp = k_refs[mp][pl.ds(k0, tk), :]
                s = lax.dot_general(qs[mp], kmp, _NT, preferred_element_type=F32) + sd_ref[...]
                if masked:
                    s = jnp.where(dist_ref[...] <= off, s, -jnp.inf)
                m_old = m_ref[mp]
                m_tile = jnp.broadcast_to(jnp.max(s, axis=-1, keepdims=True), stat) + tile_bias
                m_new = jnp.maximum(m_old, m_tile)
                corr = jnp.exp2(m_old - m_new)
                shift = tile_bias - m_new
                p = jnp.exp2(s + jnp.concatenate([shift] * (tk // LANES), axis=1))
                l_ref[mp] = l_ref[mp] * corr + jnp.broadcast_to(
                    jnp.sum(p, axis=-1, keepdims=True), stat)
                acc_ref[mp] = acc_ref[mp] * corr + jnp.dot(
                    p.astype(BF16), vt, preferred_element_type=F32)
                m_ref[mp] = m_new
            return carry

        n_full = q0 // tk
        n_all = (q0 + tq + tk - 1) // tk
        lax.fori_loop(0, n_full, functools.partial(kv_tile, masked=False), 0)
        lax.fori_loop(n_full, n_all, functools.partial(kv_tile, masked=True), 0)

        o1 = acc_ref[0] / l_ref[0]
        o2 = acc_ref[1] / l_ref[1]
        o_ref[0, pl.ds(q0, tq), :] = _rms(o1 - lam * o2, sub_gain)
        return carry

    lax.fori_loop(0, seq // tq, q_tile, 0)


def _seq_tile(seq, target):
    best = None
    for t in range(SUBLANES, min(seq, target) + 1, SUBLANES):
        if seq % t == 0:
            best = t
    assert best is not None, (seq, target)
    return best


def _attn_prompt(q, k, v, lam_vecs, subln, slopes, lam_init):
    batch, seq, width = q.shape
    heads = width // HEAD_WIDTH
    assert HEAD_WIDTH == LANES
    tq = _seq_tile(seq, 384)
    tk = 512
    seq_pad = -(-seq // tk) * tk
    blk = pl.BlockSpec((1, seq, HEAD_WIDTH), lambda b, h: (b, 0, h))
    return pl.pallas_call(
        functools.partial(_attn_prompt_kernel, seq=seq, tq=tq, tk=tk, lam_init=lam_init),
        grid=(batch, heads),
        in_specs=[
            pl.BlockSpec(memory_space=pltpu.SMEM),
            blk, blk, blk,
            pl.BlockSpec((4, HEAD_DIM), lambda b, h: (0, 0)),
            pl.BlockSpec((1, HEAD_WIDTH), lambda b, h: (0, 0)),
        ],
        out_specs=blk,
        out_shape=jax.ShapeDtypeStruct((batch, seq, width), F32),
        scratch_shapes=[
            pltpu.VMEM((seq_pad, HEAD_DIM), BF16),
            pltpu.VMEM((seq_pad, HEAD_DIM), BF16),
            pltpu.VMEM((seq_pad, HEAD_WIDTH), BF16),
            pltpu.VMEM((tq, tk), jnp.int32),
            pltpu.VMEM((tq, tk), F32),
            pltpu.VMEM((2, tq, LANES), F32),
            pltpu.VMEM((2, tq, LANES), F32),
            pltpu.VMEM((2, tq, HEAD_WIDTH), F32),
        ],
        compiler_params=_params("parallel", "parallel"),
        name="attn_prompt",
    )(slopes, q, k, v, lam_vecs, subln.reshape(1, HEAD_WIDTH))


class _Decode(NamedTuple):
    n_req: int
    n_chunks: int
    pages: int
    heads: int
    past: int
    lam_init: float
    n_inputs: int


def _decode_phases(c, cfg, dec_in, o_ref, scratch):
    pages, heads, past, lam_init = cfg.pages, cfg.heads, cfg.past, cfg.lam_init
    slopes_ref, q_ref, kn_ref, vn_ref, lam_ref, sub_ref = dec_in[:6]
    k_refs = dec_in[6:6 + pages]
    v_refs = dec_in[6 + pages:6 + 2 * pages]
    wq_ref, b0_ref, sl_ref, m_ref, l_ref, acc_ref, kc_ref, vc_ref = scratch
    n_tok = q_ref.shape[1]
    width, cols = wq_ref.shape
    group = 2 * n_tok
    page = k_refs[0].shape[1] // heads
    chunk = pages * page

    def head_group(h):
        return slice(h * group, (h + 1) * group)

    def head_cols(h):
        return slice(h * HEAD_WIDTH, (h + 1) * HEAD_WIDTH)

    def as_rows(stat):
        return jnp.broadcast_to(stat, (HEAD_WIDTH, cols)).T

    def start():
        o_ref[...] = jnp.zeros_like(o_ref)
        rep = jnp.concatenate([q_ref[0]] * (cols // n_tok), axis=0)
        r_io = lax.broadcasted_iota(jnp.int32, (cols, width), 0)
        c_io = lax.broadcasted_iota(jnp.int32, (cols, width), 1)
        keep = _idiv(c_io, HEAD_DIM) == _idiv(r_io, n_tok)
        wq_ref[...] = jnp.where(keep, rep, 0.0).T.astype(BF16)
        col = lax.broadcasted_iota(jnp.int32, (1, cols), 1)
        slope_row = jnp.zeros((1, cols), F32)
        for h in range(heads):
            slope_row = jnp.where(_idiv(col, group) == h, slopes_ref[h], slope_row)
        sl_ref[...] = slope_row
        key = lax.broadcasted_iota(jnp.int32, (chunk, cols), 0)
        tok = _imod(lax.broadcasted_iota(jnp.int32, (chunk, cols), 1), n_tok)
        b0_ref[...] = slope_row * (key - tok).astype(F32)
        m_ref[...] = jnp.full(m_ref.shape, -jnp.inf, F32)
        l_ref[...] = jnp.zeros_like(l_ref)
        acc_ref[...] = jnp.zeros_like(acc_ref)

    def softmax_part(s):
        m_old = m_ref[...]
        m_new = jnp.maximum(m_old, jnp.max(s, axis=0, keepdims=True))
        corr = jnp.exp(m_old - m_new)
        p = jnp.exp(s - m_new)
        l_ref[...] = l_ref[...] * corr + jnp.sum(p, axis=0, keepdims=True)
        m_ref[...] = m_new
        return p.T.astype(BF16), as_rows(corr)

    def values_part(pb, corr_rows, n_keys):
        span = 2 if heads % 2 == 0 else 1
        for h0 in range(0, heads, span):
            z = jnp.dot(pb[h0 * group:(h0 + span) * group],
                        vc_ref[0:n_keys, h0 * HEAD_WIDTH:(h0 + span) * HEAD_WIDTH],
                        preferred_element_type=F32)
            for i in range(span):
                hg = head_group(h0 + i)
                acc_ref[hg, :] = acc_ref[hg, :] * corr_rows[hg] + z[i * group:(i + 1) * group,
                                                                    i * HEAD_WIDTH:(i + 1) * HEAD_WIDTH]

    def scores(n_keys):
        return jnp.dot(kc_ref[0:n_keys, :], wq_ref[...], preferred_element_type=F32) + b0_ref[0:n_keys, :]

    def regroup(page_refs, dst_ref):
        for p in range(pages):
            for t in range(0, page, BF16_ROWS):
                for h in range(heads):
                    src = pl.ds(t * heads + h, BF16_ROWS, stride=heads)
                    dst = slice(p * page + t, p * page + t + BF16_ROWS)
                    dst_ref[dst, head_cols(h)] = page_refs[p][0, src, :].astype(BF16)

    def chunk_scores():
        regroup(k_refs, kc_ref)
        chunk_shift = sl_ref[...] * (c * chunk - past).astype(F32)
        return softmax_part(scores(chunk) + chunk_shift)

    def chunk_values(pb, corr_rows):
        regroup(v_refs, vc_ref)
        values_part(pb, corr_rows, chunk)

    def finish():
        pad = jnp.zeros((page - n_tok, width), F32)
        kc_ref[0:page, :] = jnp.concatenate([kn_ref[0], pad], axis=0).astype(BF16)
        vc_ref[0:page, :] = jnp.concatenate([vn_ref[0], pad], axis=0).astype(BF16)
        key = lax.broadcasted_iota(jnp.int32, (page, cols), 0)
        tok = _imod(lax.broadcasted_iota(jnp.int32, (page, cols), 1), n_tok)
        values_part(*softmax_part(jnp.where(key <= tok, scores(page), -jnp.inf)), page)

        lam = _diff_lambda(lam_ref, lam_init)
        sub_gain = sub_ref[...] * (1.0 - lam_init)
        l_rows = as_rows(l_ref[...])
        for h in range(heads):
            hg = head_group(h)
            blk = acc_ref[hg, :] / l_rows[hg]
            o_ref[0, :, head_cols(h)] = _rms(blk[0:n_tok] - lam * blk[n_tok:group], sub_gain)

    return start, chunk_scores, chunk_values, finish


def _decode_plan(grid, q, k_new, v_new, cache_k, cache_v, layer, page_table, lam_vecs, subln, slopes, lam_init):
    n_req, n_tok, width = q.shape
    n_pages = page_table.shape[1]
    depth, n_pool, page, heads, _ = cache_k.shape
    group = 2 * n_tok
    assert group % BF16_ROWS == 0 and page % BF16_ROWS == 0 and n_tok <= page
    cols = heads * group
    pages = 8
    while n_pages % pages:
        pages //= 2
    chunk = pages * page
    n_chunks = n_pages // pages
    gi, gj = grid
    assert gi * gj >= n_req * n_chunks, (grid, n_req, n_chunks)
    n_slots = -(-(gi * gj) // n_chunks)
    cache_k = cache_k.reshape(depth * n_pool, page * heads, HEAD_WIDTH)
    cache_v = cache_v.reshape(depth * n_pool, page * heads, HEAD_WIDTH)

    def slot(i, j):
        return (i * gj + j) // n_chunks

    def page_spec(p):
        def index(i, j, pt):
            step = i * gj + j
            req = jnp.minimum(step // n_chunks, n_req - 1)
            return (layer * n_pool + pt[req * n_pages + (step % n_chunks) * pages + p], 0, 0)
        return pl.BlockSpec((1, page * heads, HEAD_WIDTH), index)

    tok_spec = pl.BlockSpec((1, n_tok, width), lambda i, j, pt: (jnp.minimum(slot(i, j), n_req - 1), 0, 0))
    specs = [
        pl.BlockSpec(memory_space=pltpu.SMEM),
        tok_spec, tok_spec, tok_spec,
        pl.BlockSpec((4, HEAD_DIM), lambda i, j, pt: (0, 0)),
        pl.BlockSpec((1, HEAD_WIDTH), lambda i, j, pt: (0, 0)),
    ] + [page_spec(p) for p in range(pages)] * 2
    args = [slopes, q, k_new, v_new, lam_vecs, subln.reshape(1, HEAD_WIDTH)] + [cache_k] * pages + [cache_v] * pages
    att_spec = pl.BlockSpec((1, n_tok, width), lambda i, j, pt: (slot(i, j), 0, 0))
    att_shape = jax.ShapeDtypeStruct((n_slots, n_tok, width), F32)
    scratch = [
        pltpu.VMEM((width, cols), BF16),
        pltpu.VMEM((chunk, cols), F32),
        pltpu.VMEM((1, cols), F32),
        pltpu.VMEM((1, cols), F32),
        pltpu.VMEM((1, cols), F32),
        pltpu.VMEM((cols, HEAD_WIDTH), F32),
        pltpu.VMEM((chunk, width), BF16),
        pltpu.VMEM((chunk, width), BF16),
    ]
    cfg = _Decode(n_req=n_req, n_chunks=n_chunks, pages=pages, heads=heads, past=n_pages * page,
                  lam_init=lam_init, n_inputs=len(specs))
    return cfg, page_table.reshape(-1), specs, args, att_spec, att_shape, scratch


def _conv_kernel(*refs, taps, rc, halo, has_hist):
    if has_hist:
        x_ref, h_ref, w_ref, cb_ref, g_ref, b_ref, o_ref, wb_ref = refs
    else:
        x_ref, w_ref, cb_ref, g_ref, b_ref, o_ref, wb_ref = refs
    seq, ch = x_ref.shape[1], x_ref.shape[2]
    lead = halo - (taps - 1)
    n_out = rc // SUBLANES
    n_win = n_out + halo // SUBLANES
    n_lb = ch // LANES

    for j in range(taps):
        wb_ref[j] = jnp.broadcast_to(w_ref[j:j + 1, :], (SUBLANES, ch))
    sublane = lax.broadcasted_iota(jnp.int32, (SUBLANES, LANES), 0)

    def conv_rows(load_tile, r0):
        parts = [[None] * n_lb for _ in range(n_out)]
        for lb in range(n_lb):
            ls = slice(lb * LANES, (lb + 1) * LANES)
            xv = [load_tile(i, ls) for i in range(n_win)]
            out = [None] * n_out
            for s in range(SUBLANES):
                a_vals = [a for a in range(n_win) if lead <= SUBLANES * a + s <= lead + taps - 1]
                if not a_vals:
                    continue
                part = [None] * (n_out + (1 if s else 0))
                for a in a_vals:
                    w = wb_ref[SUBLANES * a + s - lead, :, ls]
                    for m in range(len(part)):
                        term = xv[m + a] * w
                        part[m] = term if part[m] is None else part[m] + term
                for m in range(n_out):
                    if s == 0:
                        u = part[m]
                    else:
                        u = pltpu.roll(jnp.where(sublane >= s, part[m], part[m + 1]), SUBLANES - s, 0)
                    out[m] = u if out[m] is None else out[m] + u
            for m in range(n_out):
                parts[m][lb] = out[m]
        rows = []
        for m in range(n_out):
            y = jnp.concatenate(parts[m], axis=1) + cb_ref[...]
            mu = jnp.mean(y, axis=-1, keepdims=True)
            yc = y - mu
            var = jnp.mean(yc * yc, axis=-1, keepdims=True)
            z = yc * lax.rsqrt(var + EPS) * g_ref[...] + b_ref[...]
            rows.append(z * jax.nn.sigmoid(z))
        o_ref[0, pl.ds(r0, rc), :] = jnp.concatenate(rows, axis=0).astype(o_ref.dtype)

    n_static = min(-(-halo // rc), seq // rc)
    for r in range(n_static):
        def static_tile(i, ls, r=r):
            row = r * rc - halo + i * SUBLANES
            if row >= 0:
                return x_ref[0, row:row + SUBLANES, ls]
            if has_hist:
                return h_ref[0, halo + row:halo + row + SUBLANES, ls]
            return jnp.zeros((SUBLANES, LANES), F32)

        conv_rows(static_tile, r * rc)

    def chunk(r, carry):
        r0 = pl.multiple_of(r * rc, SUBLANES)

        def tile(i, ls):
            return x_ref[0, pl.ds(pl.multiple_of(r0 - halo + i * SUBLANES, SUBLANES), SUBLANES), ls]

        conv_rows(tile, r0)
        return carry

    if seq // rc > n_static:
        lax.fori_loop(n_static, seq // rc, chunk, 0)


def _conv_group(glu, state, conv_w, conv_b, norm_g, norm_b):
    batch, seq, ch = glu.shape
    taps = conv_w.shape[0]
    halo = -(-(taps - 1) // SUBLANES) * SUBLANES
    rc = _seq_tile(seq, 48)
    has_hist = state is not None
    out_dtype = BF16 if rc % BF16_ROWS == 0 else F32
    vec = pl.BlockSpec((1, ch), lambda b: (0, 0))
    in_specs = [pl.BlockSpec((1, seq, ch), lambda b: (b, 0, 0))]
    args = [glu]
    if has_hist:
        in_specs.append(pl.BlockSpec((1, halo, ch), lambda b: (b, 0, 0)))
        args.append(jnp.pad(state, ((0, 0), (halo - (taps - 1), 0), (0, 0))))
    in_specs += [pl.BlockSpec((taps, ch), lambda b: (0, 0)), vec, vec, vec]
    args += [conv_w, conv_b.reshape(1, ch), norm_g.reshape(1, ch), norm_b.reshape(1, ch)]
    return pl.pallas_call(
        functools.partial(_conv_kernel, taps=taps, rc=rc, halo=halo, has_hist=has_hist),
        grid=(batch,),
        in_specs=in_specs,
        out_specs=pl.BlockSpec((1, seq, ch), lambda b: (b, 0, 0)),
        out_shape=jax.ShapeDtypeStruct((batch, seq, ch), out_dtype),
        scratch_shapes=[pltpu.VMEM((taps, SUBLANES, ch), F32)],
        compiler_params=_params("parallel"),
        name="conv_group",
    )(*args)


def _mix_out_kernel(x_ref, a_ref, c_ref, wa_ref, wc_ref, o_ref):
    y = jnp.dot(a_ref[...].astype(BF16), wa_ref[...], preferred_element_type=F32)
    y += jnp.dot(c_ref[...].astype(BF16), wc_ref[...], preferred_element_type=F32)
    o_ref[...] = x_ref[...] + y


def _mix_out(x, att, conv, w_out):
    rows, d = x.shape
    wa = att.shape[1]
    wc = conv.shape[1]
    assert wa == wc and w_out.shape[0] == wa + wc
    tm = _row_tile(rows, 704)
    return pl.pallas_call(
        _mix_out_kernel,
        grid=(rows // tm,),
        in_specs=[
            pl.BlockSpec((tm, d), lambda i: (i, 0)),
            pl.BlockSpec((tm, wa), lambda i: (i, 0)),
            pl.BlockSpec((tm, wc), lambda i: (i, 0)),
            pl.BlockSpec((wa, d), lambda i: (0, 0)),
            pl.BlockSpec((wc, d), lambda i: (1, 0)),
        ],
        out_specs=pl.BlockSpec((tm, d), lambda i: (i, 0)),
        out_shape=jax.ShapeDtypeStruct((rows, d), F32),
        compiler_params=_params("parallel"),
        name="mix_out",
    )(x, att, conv, w_out, w_out)


def kernel(x_prompt, x_sample, cache_k, cache_v, state_conv, page_table, meta_tokens, ffn1_norm, ffn1_w_gate, ffn1_w_up, ffn1_w_down, mix_norm, w_in, lambda_q1, lambda_k1, lambda_q2, lambda_k2, attn_subln, conv_w, conv_b, conv_norm_g, conv_norm_b, w_out, ffn2_norm, ffn2_w_gate, ffn2_w_up, ffn2_w_down, final_norm):
    depth = w_in.shape[0]
    batch, seq0, d = x_prompt.shape
    n_meta = meta_tokens.shape[0]
    seq = seq0 + n_meta
    n_req, n_tok, _ = x_sample.shape
    taps = conv_w.shape[1]
    width = w_out.shape[1] // 2
    heads = width // HEAD_WIDTH
    slopes = jnp.array([2.0 ** (-8.0 * (i + 1) / heads) for i in range(heads)], F32)

    meta = jnp.broadcast_to(meta_tokens.astype(x_prompt.dtype)[None], (batch, n_meta, d))
    xp = jnp.concatenate([meta, x_prompt], axis=1).reshape(batch * seq, d)
    xs = x_sample.reshape(n_req * n_tok, d)

    outs = [[] for _ in range(6)]
    for l in range(depth):
        lam_init = 0.8 - 0.6 * math.exp(-0.3 * l)
        lam_vecs = jnp.stack([lambda_q1[l], lambda_k1[l], lambda_q2[l], lambda_k2[l]])
        wg1, wu1, wd1 = (w[l].astype(BF16) for w in (ffn1_w_gate, ffn1_w_up, ffn1_w_down))
        wg2, wu2, wd2 = (w[l].astype(BF16) for w in (ffn2_w_gate, ffn2_w_up, ffn2_w_down))
        w_in_l = w_in[l].astype(BF16)
        w_out_l = w_out[l].astype(BF16)
        last = l == depth - 1

        xs = _ffn(xs, ffn1_norm[l], wg1, wu1, wd1)
        qs, ks, vs, glus = _mix_in(xs, mix_norm[l], w_in_l, width)
        shape_s = (n_req, n_tok, width)
        qs, ks3, vs3, glus = (a.reshape(shape_s) for a in (qs, ks, vs, glus))
        half = -(-n_req // 2)

        def decode_args(lo, hi):
            return dict(q=qs[lo:hi], k_new=ks3[lo:hi], v_new=vs3[lo:hi], cache_k=cache_k, cache_v=cache_v,
                        layer=l, page_table=page_table[lo:hi], lam_vecs=lam_vecs, subln=attn_subln[l],
                        slopes=slopes, lam_init=lam_init)

        xp, att_s0 = _ffn(xp, ffn1_norm[l], wg1, wu1, wd1, decode_args=decode_args(0, half))

        qp, kp, vp, glup = _mix_in(xp, mix_norm[l], w_in_l, width)
        shape_p = (batch, seq, width)
        att_p = _attn_prompt(qp.reshape(shape_p), kp.reshape(shape_p), vp.reshape(shape_p),
                             lam_vecs, attn_subln[l], slopes, lam_init)
        glup = glup.reshape(shape_p)
        conv_p = _conv_group(glup, None, conv_w[l], conv_b[l], conv_norm_g[l], conv_norm_b[l])
        xp = _mix_out(xp, att_p.reshape(batch * seq, width), conv_p.reshape(batch * seq, width), w_out_l)
        outs[0].append(kp.reshape(batch, seq, heads, HEAD_WIDTH))
        outs[1].append(vp.reshape(batch, seq, heads, HEAD_WIDTH))
        outs[2].append(glup[:, seq - (taps - 1):])

        fg = final_norm if last else None
        xp, att_s1 = _ffn(xp, ffn2_norm[l], wg2, wu2, wd2, fg, decode_args=decode_args(half, n_req))

        att_s = jnp.concatenate([att_s0, att_s1], axis=0)
        conv_s = _conv_group(glus, state_conv[l], conv_w[l], conv_b[l], conv_norm_g[l], conv_norm_b[l])
        xs = _mix_out(xs, att_s.reshape(n_req * n_tok, width), conv_s.reshape(n_req * n_tok, width), w_out_l)
        outs[3].append(ks.reshape(n_req, n_tok, heads, HEAD_WIDTH))
        outs[4].append(vs.reshape(n_req, n_tok, heads, HEAD_WIDTH))
        outs[5].append(jnp.concatenate([state_conv[l], glus], axis=1)[:, -(taps - 1):])
        xs = _ffn(xs, ffn2_norm[l], wg2, wu2, wd2, fg)

    y_prompt = xp.reshape(batch, seq, d)[:, n_meta:]
    y_sample = xs.reshape(n_req, n_tok, d)
    return (y_prompt, y_sample) + tuple(jnp.stack(o) for o in outs)
```

```python
import functools
import math
from typing import NamedTuple

import jax
import jax.numpy as jnp
from jax import lax
from jax.experimental import pallas as pl
from jax.experimental.pallas import tpu as pltpu

EPS = 1e-6
HEAD_DIM = 64
HEAD_WIDTH = 2 * HEAD_DIM
LANES = 128
SUBLANES = 8
BF16_ROWS = 16
LOG2E = math.log2(math.e)
VMEM_LIMIT_BYTES = 56 * 1024 * 1024

BF16 = jnp.bfloat16
F32 = jnp.float32
_NT = (((1,), (1,)), ((), ()))


def _row_tile(rows, target):
    best = None
    for t in range(BF16_ROWS, min(rows, target) + 1, BF16_ROWS):
        if rows % t == 0:
            best = t
    assert best is not None, (rows, target)
    return best


def _col_tile(cols, target):
    best = None
    for t in range(LANES, min(cols, target) + 1, LANES):
        if cols % t == 0:
            best = t
    assert best is not None, (cols, target)
    return best


def _idiv(x, n):
    if n & (n - 1) == 0:
        return lax.shift_right_logical(x, n.bit_length() - 1)
    return x // n


def _imod(x, n):
    if n & (n - 1) == 0:
        return lax.bitwise_and(x, n - 1)
    return lax.rem(x, n)


def _rms(x, gain):
    return x * lax.rsqrt(jnp.mean(x * x, axis=-1, keepdims=True) + EPS) * gain


def _params(*semantics):
    return pltpu.CompilerParams(dimension_semantics=semantics, vmem_limit_bytes=VMEM_LIMIT_BYTES)


def _ffn_kernel(*refs, final, decode, emit_bf16):
    if decode is not None:
        refs = refs[1:]
    x_ref, g_ref, wg_ref, wu_ref, wd_ref = refs[:5]
    pos = 5
    if decode is not None:
        dec_in = refs[pos:pos + decode.n_inputs]
        pos += decode.n_inputs
    if final:
        fg_ref = refs[pos]
        pos += 1
    o_ref = refs[pos]
    pos += 1
    if decode is not None:
        att_ref = refs[pos]
        pos += 1
    if emit_bf16:
        wgb_ref, wub_ref, wdb_ref = refs[pos:pos + 3]
        pos += 3
    h_ref = refs[pos]
    dec_scratch = refs[pos + 1:]
    j = pl.program_id(1)

    @pl.when(j == 0)
    def _():
        h_ref[...] = _rms(x_ref[...], g_ref[...]).astype(BF16)
        o_ref[...] = jnp.zeros_like(o_ref)

    if decode is not None:
        step = pl.program_id(0) * pl.num_programs(1) + j
        c = lax.rem(step, decode.n_chunks)
        dec_start, dec_scores, dec_values, dec_finish = _decode_phases(c, decode, dec_in, att_ref, dec_scratch)
        pl.when(c == 0)(dec_start)

    h = h_ref[...]
    wg, wu, wd = wg_ref[...], wu_ref[...], wd_ref[...]
    if emit_bf16:
        wg, wu, wd = wg.astype(BF16), wu.astype(BF16), wd.astype(BF16)
        wgb_ref[...], wub_ref[...], wdb_ref[...] = wg, wu, wd
    gate = jnp.dot(h, wg, preferred_element_type=F32)
    if decode is not None:
        dec_state = dec_scores()
    up = jnp.dot(h, wu, preferred_element_type=F32)
    if decode is not None:
        dec_values(*dec_state)
    act = (gate * jax.nn.sigmoid(gate) * up).astype(BF16)
    o_ref[...] += jnp.dot(act, wd, preferred_element_type=F32)

    if decode is not None:
        pl.when(c == decode.n_chunks - 1)(dec_finish)

    @pl.when(j == pl.num_programs(1) - 1)
    def _():
        y = x_ref[...] + 0.5 * o_ref[...]
        if final:
            y = _rms(y, fg_ref[...])
        o_ref[...] = y


def _ffn(x, gain, w_gate, w_up, w_down, final_gain=None, decode_args=None):
    rows, d = x.shape
    f = w_gate.shape[1]
    tm = _row_tile(rows, 704)
    tf = _col_tile(f, 512 if decode_args is None else 256)
    grid = (rows // tm, f // tf)
    final = final_gain is not None
    in_specs = [
        pl.BlockSpec((tm, d), lambda i, j, *_: (i, 0)),
        pl.BlockSpec((1, d), lambda i, j, *_: (0, 0)),
        pl.BlockSpec((d, tf), lambda i, j, *_: (0, j)),
        pl.BlockSpec((d, tf), lambda i, j, *_: (0, j)),
        pl.BlockSpec((tf, d), lambda i, j, *_: (j, 0)),
    ]
    args = [x, gain.reshape(1, d), w_gate, w_up, w_down]
    final_spec = [pl.BlockSpec((1, d), lambda i, j, *_: (0, 0))] if final else []
    final_arg = [final_gain.reshape(1, d)] if final else []
    out_specs = pl.BlockSpec((tm, d), lambda i, j, *_: (i, 0))
    out_shape = jax.ShapeDtypeStruct((rows, d), F32)
    scratch = [pltpu.VMEM((tm, d), BF16)]
    name = "half_ffn_final" if final else "half_ffn"
    emit_bf16 = w_gate.dtype != BF16
    if emit_bf16:
        assert decode_args is None and grid[0] == 1
        return pl.pallas_call(
            functools.partial(_ffn_kernel, final=final, decode=None, emit_bf16=True),
            grid=grid, in_specs=in_specs + final_spec,
            out_specs=[out_specs] + in_specs[2:5],
            out_shape=[out_shape] + [jax.ShapeDtypeStruct(w.shape, BF16) for w in (w_gate, w_up, w_down)],
            scratch_shapes=scratch, compiler_params=_params("arbitrary", "arbitrary"), name=name + "_cast",
        )(*args, *final_arg)
    if decode_args is None:
        return pl.pallas_call(
            functools.partial(_ffn_kernel, final=final, decode=None, emit_bf16=False),
            grid=grid, in_specs=in_specs + final_spec, out_specs=out_specs, out_shape=out_shape,
            scratch_shapes=scratch, compiler_params=_params("parallel", "arbitrary"), name=name,
        )(*args, *final_arg)

    decode, page_table, dec_specs, dec_args, att_spec, att_shape, dec_scratch = _decode_plan(
        grid, **decode_args)
    in_specs[0] = pl.BlockSpec((tm, d), lambda i, j, *_: (i, 0), pipeline_mode=pl.Buffered(1))
    grid_spec = pltpu.PrefetchScalarGridSpec(
        num_scalar_prefetch=1, grid=grid,
        in_specs=in_specs + dec_specs + final_spec,
        out_specs=[out_specs, att_spec],
        scratch_shapes=scratch + dec_scratch,
    )
    y, att = pl.pallas_call(
        functools.partial(_ffn_kernel, final=final, decode=decode, emit_bf16=False),
        grid_spec=grid_spec,
        out_shape=[out_shape, att_shape],
        compiler_params=_params("arbitrary", "arbitrary"),
        name=name + "_decode",
    )(page_table, *args, *dec_args, *final_arg)
    return y, att[:decode.n_req]


def _mix_in_kernel(x_ref, g_ref, wa_ref, wg_ref, q_ref, k_ref, v_ref, glu_ref, *rest, emit_bf16):
    if emit_bf16:
        wab_ref, wgb_ref, h_ref = rest
    else:
        (h_ref,) = rest
    j = pl.program_id(1)

    @pl.when(j == 0)
    def _():
        h_ref[...] = _rms(x_ref[...], g_ref[...]).astype(BF16)

    h = h_ref[...]
    wa = wa_ref[...]
    if emit_bf16:
        wa = wa.astype(BF16)
        wab_ref[...] = wa
    r = jnp.dot(h, wa, preferred_element_type=F32)

    @pl.when(j == 0)
    def _():
        q_ref[...] = r * (HEAD_DIM ** -0.5)

    @pl.when(j == 1)
    def _():
        k_ref[...] = r

    @pl.when(j == 2)
    def _():
        v_ref[...] = r

    @pl.when(j == 3)
    def _():
        wg = wg_ref[...]
        if emit_bf16:
            wg = wg.astype(BF16)
            wgb_ref[...] = wg
        gate = jnp.dot(h, wg, preferred_element_type=F32)
        glu_ref[...] = r * jax.nn.sigmoid(gate)


def _mix_in(x, gain, w_in, width):
    rows, d = x.shape
    tm = _row_tile(rows, 704)
    grid = (rows // tm, 4)
    out_spec = pl.BlockSpec((tm, width), lambda i, j: (i, 0))
    out_sds = jax.ShapeDtypeStruct((rows, width), F32)
    main_spec = pl.BlockSpec((d, width), lambda i, j: (0, j))
    emit_bf16 = not isinstance(w_in, tuple)
    if emit_bf16:
        assert w_in.shape[1] == 5 * width and grid[0] == 1
        weights = (w_in, w_in)
        gate_spec = pl.BlockSpec((d, width), lambda i, j: (0, 4))
        extra_specs = [main_spec, pl.BlockSpec((d, width), lambda i, j: (0, 0))]
        extra_shapes = [jax.ShapeDtypeStruct((d, 4 * width), BF16), jax.ShapeDtypeStruct((d, width), BF16)]
    else:
        weights = w_in
        gate_spec = pl.BlockSpec((d, width), lambda i, j: (0, 0))
        extra_specs, extra_shapes = [], []
    outs = pl.pallas_call(
        functools.partial(_mix_in_kernel, emit_bf16=emit_bf16),
        grid=grid,
        in_specs=[
            pl.BlockSpec((tm, d), lambda i, j: (i, 0)),
            pl.BlockSpec((1, d), lambda i, j: (0, 0)),
            main_spec,
            gate_spec,
        ],
        out_specs=[out_spec] * 4 + extra_specs,
        out_shape=[out_sds] * 4 + extra_shapes,
        scratch_shapes=[pltpu.VMEM((tm, d), BF16)],
        compiler_params=_params("arbitrary" if emit_bf16 else "parallel", "arbitrary"),
        name="mix_in_cast" if emit_bf16 else "mix_in",
    )(x, gain.reshape(1, d), *weights)
    if emit_bf16:
        return outs[:4], (outs[4], outs[5])
    return outs


def _diff_lambda(lam_ref, lam_init):
    lv = lam_ref[...]
    d1 = jnp.sum(lv[0:1] * lv[1:2], axis=-1, keepdims=True)
    d2 = jnp.sum(lv[2:3] * lv[3:4], axis=-1, keepdims=True)
    return jnp.exp(d1) - jnp.exp(d2) + lam_init


def _attn_prompt_kernel(slopes_ref, q_ref, k_ref, v_ref, lam_ref, sub_ref, o_ref,
                        k1_ref, k2_ref, vb_ref, dist_ref, sd_ref, m_ref, l_ref, acc_ref,
                        *, seq, tq, tk, hps, lam_init):
    head0 = pl.program_id(1) * hps
    seq_pad = vb_ref.shape[1]
    k_refs = (k1_ref, k2_ref)

    dist = (lax.broadcasted_iota(jnp.int32, (tq, tk), 1)
            - lax.broadcasted_iota(jnp.int32, (tq, tk), 0))
    dist_ref[...] = dist
    slope2 = []
    for hh in range(hps):
        cols = slice(hh * HEAD_WIDTH, (hh + 1) * HEAD_WIDTH)
        k = k_ref[0, :, cols]
        k1_ref[hh, 0:seq, :] = k[:, :HEAD_DIM].astype(BF16)
        k2_ref[hh, 0:seq, :] = k[:, HEAD_DIM:].astype(BF16)
        vb_ref[hh, 0:seq, :] = v_ref[0, :, cols].astype(BF16)
        if seq_pad > seq:
            k1_ref[hh, seq:seq_pad, :] = jnp.zeros((seq_pad - seq, HEAD_DIM), BF16)
            k2_ref[hh, seq:seq_pad, :] = jnp.zeros((seq_pad - seq, HEAD_DIM), BF16)
            vb_ref[hh, seq:seq_pad, :] = jnp.zeros((seq_pad - seq, HEAD_WIDTH), BF16)
        slope2.append(slopes_ref[head0 + hh] * LOG2E)
        sd_ref[hh] = slope2[hh] * dist.astype(F32)
    lam = _diff_lambda(lam_ref, lam_init)
    sub_gain = sub_ref[...] * (1.0 - lam_init)
    stat = (tq, LANES)

    def q_tile(i, carry):
        q0 = pl.multiple_of(i * tq, SUBLANES)
        qt = q_ref[0, pl.ds(q0, tq), :] * LOG2E
        qs = [qt[:, n * HEAD_DIM:(n + 1) * HEAD_DIM].astype(BF16) for n in range(2 * hps)]
        m_ref[...] = jnp.full(m_ref.shape, -jnp.inf, F32)
        l_ref[...] = jnp.zeros_like(l_ref)
        acc_ref[...] = jnp.zeros_like(acc_ref)

        def kv_tile(j, carry, masked):
            k0 = pl.multiple_of(j * tk, tk)
            off = q0 - k0
            for hh in range(hps):
                vt = vb_ref[hh, pl.ds(k0, tk), :]
                tile_bias = -slope2[hh] * off.astype(F32)
                for mp in range(2):
                    n = 2 * hh + mp
                    kmp = k_refs[mp][hh, pl.ds(k0, tk), :]
                    s = lax.dot_general(qs[n], kmp, _NT, preferred_element_type=F32) + sd_ref[hh]
                    if masked:
                        s = jnp.where(dist_ref[...] <= off, s, -jnp.inf)
                    m_old = m_ref[n]
                    m_tile = jnp.broadcast_to(jnp.max(s, axis=-1, keepdims=True), stat) + tile_bias
                    m_new = jnp.maximum(m_old, m_tile)
                    corr = jnp.exp2(m_old - m_new)
                    shift = tile_bias - m_new
                    p = jnp.exp2(s + jnp.concatenate([shift] * (tk // LANES), axis=1))
                    l_ref[n] = l_ref[n] * corr + jnp.broadcast_to(
                        jnp.sum(p, axis=-1, keepdims=True), stat)
                    acc_ref[n] = acc_ref[n] * corr + jnp.dot(
                        p.astype(BF16), vt, preferred_element_type=F32)
                    m_ref[n] = m_new
            return carry

        n_full = q0 // tk
        n_all = (q0 + tq + tk - 1) // tk
        lax.fori_loop(0, n_full, functools.partial(kv_tile, masked=False), 0)
        lax.fori_loop(n_full, n_all, functools.partial(kv_tile, masked=True), 0)

        for hh in range(hps):
            o1 = acc_ref[2 * hh] / l_ref[2 * hh]
            o2 = acc_ref[2 * hh + 1] / l_ref[2 * hh + 1]
            o_ref[0, pl.ds(q0, tq), hh * HEAD_WIDTH:(hh + 1) * HEAD_WIDTH] = _rms(o1 - lam * o2, sub_gain)
        return carry

    lax.fori_loop(0, seq // tq, q_tile, 0)


def _seq_tile(seq, target):
    best = None
    for t in range(SUBLANES, min(seq, target) + 1, SUBLANES):
        if seq % t == 0:
            best = t
    assert best is not None, (seq, target)
    return best


def _attn_prompt(q, k, v, lam_vecs, subln, slopes, lam_init):
    batch, seq, width = q.shape
    heads = width // HEAD_WIDTH
    assert HEAD_WIDTH == LANES
    hps = 2 if heads % 2 == 0 else 1
    tq = _seq_tile(seq, 384)
    tk = 512
    seq_pad = -(-seq // tk) * tk
    blk = pl.BlockSpec((1, seq, hps * HEAD_WIDTH), lambda b, h: (b, 0, h))
    return pl.pallas_call(
        functools.partial(_attn_prompt_kernel, seq=seq, tq=tq, tk=tk, hps=hps, lam_init=lam_init),
        grid=(batch, heads // hps),
        in_specs=[
            pl.BlockSpec(memory_space=pltpu.SMEM),
            blk, blk, blk,
            pl.BlockSpec((4, HEAD_DIM), lambda b, h: (0, 0)),
            pl.BlockSpec((1, HEAD_WIDTH), lambda b, h: (0, 0)),
        ],
        out_specs=blk,
        out_shape=jax.ShapeDtypeStruct((batch, seq, width), F32),
        scratch_shapes=[
            pltpu.VMEM((hps, seq_pad, HEAD_DIM), BF16),
            pltpu.VMEM((hps, seq_pad, HEAD_DIM), BF16),
            pltpu.VMEM((hps, seq_pad, HEAD_WIDTH), BF16),
            pltpu.VMEM((tq, tk), jnp.int32),
            pltpu.VMEM((hps, tq, tk), F32),
            pltpu.VMEM((2 * hps, tq, LANES), F32),
            pltpu.VMEM((2 * hps, tq, LANES), F32),
            pltpu.VMEM((2 * hps, tq, HEAD_WIDTH), F32),
        ],
        compiler_params=_params("parallel", "parallel"),
        name="attn_prompt",
    )(slopes, q, k, v, lam_vecs, subln.reshape(1, HEAD_WIDTH))


class _Decode(NamedTuple):
    n_req: int
    n_chunks: int
    pages: int
    heads: int
    past: int
    lam_init: float
    n_inputs: int


def _decode_phases(c, cfg, dec_in, o_ref, scratch):
    pages, heads, past, lam_init = cfg.pages, cfg.heads, cfg.past, cfg.lam_init
    slopes_ref, q_ref, kn_ref, vn_ref, lam_ref, sub_ref = dec_in[:6]
    k_refs = dec_in[6:6 + pages]
    v_refs = dec_in[6 + pages:6 + 2 * pages]
    wq_ref, b0_ref, sl_ref, m_ref, l_ref, acc_ref, kc_ref, vc_ref = scratch
    n_tok = q_ref.shape[1]
    width, cols = wq_ref.shape
    group = 2 * n_tok
    page = k_refs[0].shape[1] // heads
    chunk = pages * page

    def head_group(h):
        return slice(h * group, (h + 1) * group)

    def head_cols(h):
        return slice(h * HEAD_WIDTH, (h + 1) * HEAD_WIDTH)

    def as_rows(stat):
        return jnp.broadcast_to(stat, (HEAD_WIDTH, cols)).T

    def start():
        o_ref[...] = jnp.zeros_like(o_ref)
        rep = jnp.concatenate([q_ref[0]] * (cols // n_tok), axis=0)
        r_io = lax.broadcasted_iota(jnp.int32, (cols, width), 0)
        c_io = lax.broadcasted_iota(jnp.int32, (cols, width), 1)
        keep = _idiv(c_io, HEAD_DIM) == _idiv(r_io, n_tok)
        wq_ref[...] = jnp.where(keep, rep, 0.0).T.astype(BF16)
        col = lax.broadcasted_iota(jnp.int32, (1, cols), 1)
        slope_row = jnp.zeros((1, cols), F32)
        for h in range(heads):
            slope_row = jnp.where(_idiv(col, group) == h, slopes_ref[h], slope_row)
        sl_ref[...] = slope_row
        key = lax.broadcasted_iota(jnp.int32, (chunk, cols), 0)
        tok = _imod(lax.broadcasted_iota(jnp.int32, (chunk, cols), 1), n_tok)
        b0_ref[...] = slope_row * (key - tok).astype(F32)
        m_ref[...] = jnp.full(m_ref.shape, -jnp.inf, F32)
        l_ref[...] = jnp.zeros_like(l_ref)
        acc_ref[...] = jnp.zeros_like(acc_ref)

    def softmax_part(s):
        m_old = m_ref[...]
        m_new = jnp.maximum(m_old, jnp.max(s, axis=0, keepdims=True))
        corr = jnp.exp(m_old - m_new)
        p = jnp.exp(s - m_new)
        l_ref[...] = l_ref[...] * corr + jnp.sum(p, axis=0, keepdims=True)
        m_ref[...] = m_new
        return p.T.astype(BF16), as_rows(corr)

    def values_part(pb, corr_rows, n_keys):
        span = 2 if heads % 2 == 0 else 1
        for h0 in range(0, heads, span):
            z = jnp.dot(pb[h0 * group:(h0 + span) * group],
                        vc_ref[0:n_keys, h0 * HEAD_WIDTH:(h0 + span) * HEAD_WIDTH],
                        preferred_element_type=F32)
            for i in range(span):
                hg = head_group(h0 + i)
                acc_ref[hg, :] = acc_ref[hg, :] * corr_rows[hg] + z[i * group:(i + 1) * group,
                                                                    i * HEAD_WIDTH:(i + 1) * HEAD_WIDTH]

    def scores(n_keys):
        return jnp.dot(kc_ref[0:n_keys, :], wq_ref[...], preferred_element_type=F32) + b0_ref[0:n_keys, :]

    def regroup(page_refs, dst_ref):
        for p in range(pages):
            for t in range(0, page, BF16_ROWS):
                for h in range(heads):
                    src = pl.ds(t * heads + h, BF16_ROWS, stride=heads)
                    dst = slice(p * page + t, p * page + t + BF16_ROWS)
                    dst_ref[dst, head_cols(h)] = page_refs[p][0, src, :].astype(BF16)

    def chunk_scores():
        regroup(k_refs, kc_ref)
        chunk_shift = sl_ref[...] * (c * chunk - past).astype(F32)
        return softmax_part(scores(chunk) + chunk_shift)

    def chunk_values(pb, corr_rows):
        regroup(v_refs, vc_ref)
        values_part(pb, corr_rows, chunk)

    def finish():
        pad = jnp.zeros((page - n_tok, width), F32)
        kc_ref[0:page, :] = jnp.concatenate([kn_ref[0], pad], axis=0).astype(BF16)
        vc_ref[0:page, :] = jnp.concatenate([vn_ref[0], pad], axis=0).astype(BF16)
        key = lax.broadcasted_iota(jnp.int32, (page, cols), 0)
        tok = _imod(lax.broadcasted_iota(jnp.int32, (page, cols), 1), n_tok)
        values_part(*softmax_part(jnp.where(key <= tok, scores(page), -jnp.inf)), page)

        lam = _diff_lambda(lam_ref, lam_init)
        sub_gain = sub_ref[...] * (1.0 - lam_init)
        l_rows = as_rows(l_ref[...])
        for h in range(heads):
            hg = head_group(h)
            blk = acc_ref[hg, :] / l_rows[hg]
            o_ref[0, :, head_cols(h)] = _rms(blk[0:n_tok] - lam * blk[n_tok:group], sub_gain)

    return start, chunk_scores, chunk_values, finish


def _decode_plan(grid, q, k_new, v_new, cache_k, cache_v, layer, page_table, lam_vecs, subln, slopes, lam_init):
    n_req, n_tok, width = q.shape
    n_pages = page_table.shape[1]
    depth, n_pool, page, heads, _ = cache_k.shape
    group = 2 * n_tok
    assert group % BF16_ROWS == 0 and page % BF16_ROWS == 0 and n_tok <= page
    cols = heads * group
    pages = 8
    while n_pages % pages:
        pages //= 2
    chunk = pages * page
    n_chunks = n_pages // pages
    gi, gj = grid
    assert gi * gj >= n_req * n_chunks, (grid, n_req, n_chunks)
    n_slots = -(-(gi * gj) // n_chunks)
    cache_k = cache_k.reshape(depth * n_pool, page * heads, HEAD_WIDTH)
    cache_v = cache_v.reshape(depth * n_pool, page * heads, HEAD_WIDTH)

    def slot(i, j):
        return (i * gj + j) // n_chunks

    def page_spec(p):
        def index(i, j, pt):
            step = i * gj + j
            req = jnp.minimum(step // n_chunks, n_req - 1)
            return (layer * n_pool + pt[req * n_pages + (step % n_chunks) * pages + p], 0, 0)
        return pl.BlockSpec((1, page * heads, HEAD_WIDTH), index)

    tok_spec = pl.BlockSpec((1, n_tok, width), lambda i, j, pt: (jnp.minimum(slot(i, j), n_req - 1), 0, 0))
    specs = [
        pl.BlockSpec(memory_space=pltpu.SMEM),
        tok_spec, tok_spec, tok_spec,
        pl.BlockSpec((4, HEAD_DIM), lambda i, j, pt: (0, 0)),
        pl.BlockSpec((1, HEAD_WIDTH), lambda i, j, pt: (0, 0)),
    ] + [page_spec(p) for p in range(pages)] * 2
    args = [slopes, q, k_new, v_new, lam_vecs, subln.reshape(1, HEAD_WIDTH)] + [cache_k] * pages + [cache_v] * pages
    att_spec = pl.BlockSpec((1, n_tok, width), lambda i, j, pt: (slot(i, j), 0, 0))
    att_shape = jax.ShapeDtypeStruct((n_slots, n_tok, width), F32)
    scratch = [
        pltpu.VMEM((width, cols), BF16),
        pltpu.VMEM((chunk, cols), F32),
        pltpu.VMEM((1, cols), F32),
        pltpu.VMEM((1, cols), F32),
        pltpu.VMEM((1, cols), F32),
        pltpu.VMEM((cols, HEAD_WIDTH), F32),
        pltpu.VMEM((chunk, width), BF16),
        pltpu.VMEM((chunk, width), BF16),
    ]
    cfg = _Decode(n_req=n_req, n_chunks=n_chunks, pages=pages, heads=heads, past=n_pages * page,
                  lam_init=lam_init, n_inputs=len(specs))
    return cfg, page_table.reshape(-1), specs, args, att_spec, att_shape, scratch


def _conv_kernel(*refs, taps, rc, halo, has_hist):
    if has_hist:
        x_ref, h_ref, w_ref, cb_ref, g_ref, b_ref, o_ref, wb_ref = refs
    else:
        x_ref, w_ref, cb_ref, g_ref, b_ref, o_ref, wb_ref = refs
    seq, ch = x_ref.shape[1], x_ref.shape[2]
    lead = halo - (taps - 1)
    n_out = rc // SUBLANES
    n_win = n_out + halo // SUBLANES
    n_lb = ch // LANES

    for j in range(taps):
        wb_ref[j] = jnp.broadcast_to(w_ref[j:j + 1, :], (SUBLANES, ch))
    sublane = lax.broadcasted_iota(jnp.int32, (SUBLANES, LANES), 0)

    def conv_rows(load_tile, r0):
        parts = [[None] * n_lb for _ in range(n_out)]
        for lb in range(n_lb):
            ls = slice(lb * LANES, (lb + 1) * LANES)
            xv = [load_tile(i, ls) for i in range(n_win)]
            out = [None] * n_out
            for s in range(SUBLANES):
                a_vals = [a for a in range(n_win) if lead <= SUBLANES * a + s <= lead + taps - 1]
                if not a_vals:
                    continue
                part = [None] * (n_out + (1 if s else 0))
                for a in a_vals:
                    w = wb_ref[SUBLANES * a + s - lead, :, ls]
                    for m in range(len(part)):
                        term = xv[m + a] * w
                        part[m] = term if part[m] is None else part[m] + term
                for m in range(n_out):
                    if s == 0:
                        u = part[m]
                    else:
                        u = pltpu.roll(jnp.where(sublane >= s, part[m], part[m + 1]), SUBLANES - s, 0)
                    out[m] = u if out[m] is None else out[m] + u
            for m in range(n_out):
                parts[m][lb] = out[m]
        rows = []
        for m in range(n_out):
            y = jnp.concatenate(parts[m], axis=1) + cb_ref[...]
            mu = jnp.mean(y, axis=-1, keepdims=True)
            yc = y - mu
            var = jnp.mean(yc * yc, axis=-1, keepdims=True)
            z = yc * lax.rsqrt(var + EPS) * g_ref[...] + b_ref[...]
            rows.append(z * jax.nn.sigmoid(z))
        o_ref[0, pl.ds(r0, rc), :] = jnp.concatenate(rows, axis=0).astype(o_ref.dtype)

    n_static = min(-(-halo // rc), seq // rc)
    for r in range(n_static):
        def static_tile(i, ls, r=r):
            row = r * rc - halo + i * SUBLANES
            if row >= 0:
                return x_ref[0, row:row + SUBLANES, ls]
            if has_hist:
                return h_ref[0, halo + row:halo + row + SUBLANES, ls]
            return jnp.zeros((SUBLANES, LANES), F32)

        conv_rows(static_tile, r * rc)

    def chunk(r, carry):
        r0 = pl.multiple_of(r * rc, SUBLANES)

        def tile(i, ls):
            return x_ref[0, pl.ds(pl.multiple_of(r0 - halo + i * SUBLANES, SUBLANES), SUBLANES), ls]

        conv_rows(tile, r0)
        return carry

    if seq // rc > n_static:
        lax.fori_loop(n_static, seq // rc, chunk, 0)


def _conv_group(glu, state, conv_w, conv_b, norm_g, norm_b):
    batch, seq, ch = glu.shape
    taps = conv_w.shape[0]
    halo = -(-(taps - 1) // SUBLANES) * SUBLANES
    rc = _seq_tile(seq, 48)
    has_hist = state is not None
    out_dtype = BF16 if rc % BF16_ROWS == 0 else F32
    vec = pl.BlockSpec((1, ch), lambda b: (0, 0))
    in_specs = [pl.BlockSpec((1, seq, ch), lambda b: (b, 0, 0))]
    args = [glu]
    if has_hist:
        in_specs.append(pl.BlockSpec((1, halo, ch), lambda b: (b, 0, 0)))
        args.append(jnp.pad(state, ((0, 0), (halo - (taps - 1), 0), (0, 0))))
    in_specs += [pl.BlockSpec((taps, ch), lambda b: (0, 0)), vec, vec, vec]
    args += [conv_w, conv_b.reshape(1, ch), norm_g.reshape(1, ch), norm_b.reshape(1, ch)]
    return pl.pallas_call(
        functools.partial(_conv_kernel, taps=taps, rc=rc, halo=halo, has_hist=has_hist),
        grid=(batch,),
        in_specs=in_specs,
        out_specs=pl.BlockSpec((1, seq, ch), lambda b: (b, 0, 0)),
        out_shape=jax.ShapeDtypeStruct((batch, seq, ch), out_dtype),
        scratch_shapes=[pltpu.VMEM((taps, SUBLANES, ch), F32)],
        compiler_params=_params("parallel"),
        name="conv_group",
    )(*args)


def _mix_out_kernel(x_ref, a_ref, c_ref, wa_ref, wc_ref, o_ref):
    y = jnp.dot(a_ref[...].astype(BF16), wa_ref[...], preferred_element_type=F32)
    y += jnp.dot(c_ref[...].astype(BF16), wc_ref[...], preferred_element_type=F32)
    o_ref[...] = x_ref[...] + y


def _mix_out(x, att, conv, w_out):
    rows, d = x.shape
    wa = att.shape[1]
    wc = conv.shape[1]
    assert wa == wc and w_out.shape[0] == wa + wc
    tm = _row_tile(rows, 704)
    return pl.pallas_call(
        _mix_out_kernel,
        grid=(rows // tm,),
        in_specs=[
            pl.BlockSpec((tm, d), lambda i: (i, 0)),
            pl.BlockSpec((tm, wa), lambda i: (i, 0)),
            pl.BlockSpec((tm, wc), lambda i: (i, 0)),
            pl.BlockSpec((wa, d), lambda i: (0, 0)),
            pl.BlockSpec((wc, d), lambda i: (1, 0)),
        ],
        out_specs=pl.BlockSpec((tm, d), lambda i: (i, 0)),
        out_shape=jax.ShapeDtypeStruct((rows, d), F32),
        compiler_params=_params("parallel"),
        name="mix_out",
    )(x, att, conv, w_out, w_out)


def kernel(x_prompt, x_sample, cache_k, cache_v, state_conv, page_table, meta_tokens, ffn1_norm, ffn1_w_gate, ffn1_w_up, ffn1_w_down, mix_norm, w_in, lambda_q1, lambda_k1, lambda_q2, lambda_k2, attn_subln, conv_w, conv_b, conv_norm_g, conv_norm_b, w_out, ffn2_norm, ffn2_w_gate, ffn2_w_up, ffn2_w_down, final_norm):
    depth = w_in.shape[0]
    batch, seq0, d = x_prompt.shape
    n_meta = meta_tokens.shape[0]
    seq = seq0 + n_meta
    n_req, n_tok, _ = x_sample.shape
    taps = conv_w.shape[1]
    width = w_out.shape[1] // 2
    heads = width // HEAD_WIDTH
    slopes = jnp.array([2.0 ** (-8.0 * (i + 1) / heads) for i in range(heads)], F32)

    meta = jnp.broadcast_to(meta_tokens.astype(x_prompt.dtype)[None], (batch, n_meta, d))
    xp = jnp.concatenate([meta, x_prompt], axis=1).reshape(batch * seq, d)
    xs = x_sample.reshape(n_req * n_tok, d)

    outs = [[] for _ in range(6)]
    for l in range(depth):
        lam_init = 0.8 - 0.6 * math.exp(-0.3 * l)
        lam_vecs = jnp.stack([lambda_q1[l], lambda_k1[l], lambda_q2[l], lambda_k2[l]])
        wg2, wu2, wd2 = (w[l].astype(BF16) for w in (ffn2_w_gate, ffn2_w_up, ffn2_w_down))
        w_out_l = w_out[l].astype(BF16)
        last = l == depth - 1

        xs, wg1, wu1, wd1 = _ffn(xs, ffn1_norm[l], ffn1_w_gate[l], ffn1_w_up[l], ffn1_w_down[l])
        (qs, ks, vs, glus), w_in_l = _mix_in(xs, mix_norm[l], w_in[l], width)
        shape_s = (n_req, n_tok, width)
        qs, ks3, vs3, glus = (a.reshape(shape_s) for a in (qs, ks, vs, glus))
        half = -(-n_req // 2)

        def decode_args(lo, hi):
            return dict(q=qs[lo:hi], k_new=ks3[lo:hi], v_new=vs3[lo:hi], cache_k=cache_k, cache_v=cache_v,
                        layer=l, page_table=page_table[lo:hi], lam_vecs=lam_vecs, subln=attn_subln[l],
                        slopes=slopes, lam_init=lam_init)

        xp, att_s0 = _ffn(xp, ffn1_norm[l], wg1, wu1, wd1, decode_args=decode_args(0, half))

        qp, kp, vp, glup = _mix_in(xp, mix_norm[l], w_in_l, width)
        shape_p = (batch, seq, width)
        att_p = _attn_prompt(qp.reshape(shape_p), kp.reshape(shape_p), vp.reshape(shape_p),
                             lam_vecs, attn_subln[l], slopes, lam_init)
        glup = glup.reshape(shape_p)
        conv_p = _conv_group(glup, None, conv_w[l], conv_b[l], conv_norm_g[l], conv_norm_b[l])
        xp = _mix_out(xp, att_p.reshape(batch * seq, width), conv_p.reshape(batch * seq, width), w_out_l)
        outs[0].append(kp.reshape(batch, seq, heads, HEAD_WIDTH))
        outs[1].append(vp.reshape(batch, seq, heads, HEAD_WIDTH))
        outs[2].append(glup[:, seq - (taps - 1):])

        fg = final_norm if last else None
        xp, att_s1 = _ffn(xp, ffn2_norm[l], wg2, wu2, wd2, fg, decode_args=decode_args(half, n_req))

        att_s = jnp.concatenate([att_s0, att_s1], axis=0)
        conv_s = _conv_group(glus, state_conv[l], conv_w[l], conv_b[l], conv_norm_g[l], conv_norm_b[l])
        xs = _mix_out(xs, att_s.reshape(n_req * n_tok, width), conv_s.reshape(n_req * n_tok, width), w_out_l)
        outs[3].append(ks.reshape(n_req, n_tok, heads, HEAD_WIDTH))
        outs[4].append(vs.reshape(n_req, n_tok, heads, HEAD_WIDTH))
        outs[5].append(jnp.concatenate([state_conv[l], glus], axis=1)[:, -(taps - 1):])
        xs = _ffn(xs, ffn2_norm[l], wg2, wu2, wd2, fg)

    y_prompt = xp.reshape(batch, seq, d)[:, n_meta:]
    y_sample = xs.reshape(n_req, n_tok, d)
    return (y_prompt, y_sample) + tuple(jnp.stack(o) for o in outs)
```

```python
import functools
import math
from typing import NamedTuple

import jax
import jax.numpy as jnp
from jax import lax
from jax.experimental import pallas as pl
from jax.experimental.pallas import tpu as pltpu

EPS = 1e-6
HEAD_DIM = 64
HEAD_WIDTH = 2 * HEAD_DIM
LANES = 128
SUBLANES = 8
BF16_ROWS = 16
LOG2E = math.log2(math.e)
VMEM_LIMIT_BYTES = 56 * 1024 * 1024

BF16 = jnp.bfloat16
F32 = jnp.float32
_NT = (((1,), (1,)), ((), ()))


def _row_tile(rows, target):
    best = None
    for t in range(BF16_ROWS, min(rows, target) + 1, BF16_ROWS):
        if rows % t == 0:
            best = t
    assert best is not None, (rows, target)
    return best


def _col_tile(cols, target):
    best = None
    for t in range(LANES, min(cols, target) + 1, LANES):
        if cols % t == 0:
            best = t
    assert best is not None, (cols, target)
    return best


def _idiv(x, n):
    if n & (n - 1) == 0:
        return lax.shift_right_logical(x, n.bit_length() - 1)
    return x // n


def _imod(x, n):
    if n & (n - 1) == 0:
        return lax.bitwise_and(x, n - 1)
    return lax.rem(x, n)


def _rms(x, gain):
    return x * lax.rsqrt(jnp.mean(x * x, axis=-1, keepdims=True) + EPS) * gain


def _params(*semantics):
    return pltpu.CompilerParams(dimension_semantics=semantics, vmem_limit_bytes=VMEM_LIMIT_BYTES)


def _ffn_kernel(*refs, final, decode, emit_bf16):
    if decode is not None:
        refs = refs[1:]
    x_ref, g_ref, wg_ref, wu_ref, wd_ref = refs[:5]
    pos = 5
    if decode is not None:
        dec_in = refs[pos:pos + decode.n_inputs]
        pos += decode.n_inputs
    if final:
        fg_ref = refs[pos]
        pos += 1
    o_ref = refs[pos]
    pos += 1
    if decode is not None:
        att_ref = refs[pos]
        pos += 1
    if emit_bf16:
        wgb_ref, wub_ref, wdb_ref = refs[pos:pos + 3]
        pos += 3
    h_ref = refs[pos]
    dec_scratch = refs[pos + 1:]
    j = pl.program_id(1)

    @pl.when(j == 0)
    def _():
        h_ref[...] = _rms(x_ref[...], g_ref[...]).astype(BF16)
        o_ref[...] = jnp.zeros_like(o_ref)

    if decode is not None:
        step = pl.program_id(0) * pl.num_programs(1) + j
        c = lax.rem(step, decode.n_chunks)
        dec_start, dec_scores, dec_values, dec_finish = _decode_phases(c, decode, dec_in, att_ref, dec_scratch)
        pl.when(c == 0)(dec_start)

    h = h_ref[...]
    wg, wu, wd = wg_ref[...], wu_ref[...], wd_ref[...]
    if emit_bf16:
        wg, wu, wd = wg.astype(BF16), wu.astype(BF16), wd.astype(BF16)
        wgb_ref[...], wub_ref[...], wdb_ref[...] = wg, wu, wd
    gate = jnp.dot(h, wg, preferred_element_type=F32)
    if decode is not None:
        dec_state = dec_scores()
    up = jnp.dot(h, wu, preferred_element_type=F32)
    if decode is not None:
        for part in range(decode.value_parts):
            dec_values(part, *dec_state)
    act = (gate * jax.nn.sigmoid(gate) * up).astype(BF16)
    o_ref[...] += jnp.dot(act, wd, preferred_element_type=F32)

    if decode is not None:
        pl.when(c == decode.n_chunks - 1)(dec_finish)

    @pl.when(j == pl.num_programs(1) - 1)
    def _():
        y = x_ref[...] + 0.5 * o_ref[...]
        if final:
            y = _rms(y, fg_ref[...])
        o_ref[...] = y


def _ffn(x, gain, w_gate, w_up, w_down, final_gain=None, decode_args=None):
    rows, d = x.shape
    f = w_gate.shape[1]
    tm = _row_tile(rows, 704)
    tf = _col_tile(f, 512 if decode_args is None else 256)
    grid = (rows // tm, f // tf)
    final = final_gain is not None
    in_specs = [
        pl.BlockSpec((tm, d), lambda i, j, *_: (i, 0)),
        pl.BlockSpec((1, d), lambda i, j, *_: (0, 0)),
        pl.BlockSpec((d, tf), lambda i, j, *_: (0, j)),
        pl.BlockSpec((d, tf), lambda i, j, *_: (0, j)),
        pl.BlockSpec((tf, d), lambda i, j, *_: (j, 0)),
    ]
    args = [x, gain.reshape(1, d), w_gate, w_up, w_down]
    final_spec = [pl.BlockSpec((1, d), lambda i, j, *_: (0, 0))] if final else []
    final_arg = [final_gain.reshape(1, d)] if final else []
    out_specs = pl.BlockSpec((tm, d), lambda i, j, *_: (i, 0))
    out_shape = jax.ShapeDtypeStruct((rows, d), F32)
    scratch = [pltpu.VMEM((tm, d), BF16)]
    name = "half_ffn_final" if final else "half_ffn"
    emit_bf16 = w_gate.dtype != BF16
    if emit_bf16:
        assert decode_args is None and grid[0] == 1
        return pl.pallas_call(
            functools.partial(_ffn_kernel, final=final, decode=None, emit_bf16=True),
            grid=grid, in_specs=in_specs + final_spec,
            out_specs=[out_specs] + in_specs[2:5],
            out_shape=[out_shape] + [jax.ShapeDtypeStruct(w.shape, BF16) for w in (w_gate, w_up, w_down)],
            scratch_shapes=scratch, compiler_params=_params("arbitrary", "arbitrary"), name=name + "_cast",
        )(*args, *final_arg)
    if decode_args is None:
        return pl.pallas_call(
            functools.partial(_ffn_kernel, final=final, decode=None, emit_bf16=False),
            grid=grid, in_specs=in_specs + final_spec, out_specs=out_specs, out_shape=out_shape,
            scratch_shapes=scratch, compiler_params=_params("parallel", "arbitrary"), name=name,
        )(*args, *final_arg)

    decode, page_table, dec_specs, dec_args, att_spec, att_shape, dec_scratch = _decode_plan(
        grid, **decode_args)
    in_specs[0] = pl.BlockSpec((tm, d), lambda i, j, *_: (i, 0), pipeline_mode=pl.Buffered(1))
    grid_spec = pltpu.PrefetchScalarGridSpec(
        num_scalar_prefetch=1, grid=grid,
        in_specs=in_specs + dec_specs + final_spec,
        out_specs=[out_specs, att_spec],
        scratch_shapes=scratch + dec_scratch,
    )
    y, att = pl.pallas_call(
        functools.partial(_ffn_kernel, final=final, decode=decode, emit_bf16=False),
        grid_spec=grid_spec,
        out_shape=[out_shape, att_shape],
        compiler_params=_params("arbitrary", "arbitrary"),
        name=name + "_decode",
    )(page_table, *args, *dec_args, *final_arg)
    return y, att[:decode.n_req]


def _mix_in_kernel(x_ref, g_ref, wa_ref, wg_ref, q_ref, k_ref, v_ref, glu_ref, *rest, emit_bf16):
    if emit_bf16:
        wab_ref, wgb_ref, h_ref = rest
    else:
        (h_ref,) = rest
    j = pl.program_id(1)

    @pl.when(j == 0)
    def _():
        h_ref[...] = _rms(x_ref[...], g_ref[...]).astype(BF16)

    h = h_ref[...]
    wa = wa_ref[...]
    if emit_bf16:
        wa = wa.astype(BF16)
        wab_ref[...] = wa
    r = jnp.dot(h, wa, preferred_element_type=F32)

    @pl.when(j == 0)
    def _():
        q_ref[...] = r * (HEAD_DIM ** -0.5)

    @pl.when(j == 1)
    def _():
        k_ref[...] = r

    @pl.when(j == 2)
    def _():
        v_ref[...] = r

    @pl.when(j == 3)
    def _():
        wg = wg_ref[...]
        if emit_bf16:
            wg = wg.astype(BF16)
            wgb_ref[...] = wg
        gate = jnp.dot(h, wg, preferred_element_type=F32)
        glu_ref[...] = r * jax.nn.sigmoid(gate)


def _mix_in(x, gain, w_in, width):
    rows, d = x.shape
    tm = _row_tile(rows, 704)
    grid = (rows // tm, 4)
    out_spec = pl.BlockSpec((tm, width), lambda i, j: (i, 0))
    out_sds = jax.ShapeDtypeStruct((rows, width), F32)
    main_spec = pl.BlockSpec((d, width), lambda i, j: (0, j))
    emit_bf16 = not isinstance(w_in, tuple)
    if emit_bf16:
        assert w_in.shape[1] == 5 * width and grid[0] == 1
        weights = (w_in, w_in)
        gate_spec = pl.BlockSpec((d, width), lambda i, j: (0, 4))
        extra_specs = [main_spec, pl.BlockSpec((d, width), lambda i, j: (0, 0))]
        extra_shapes = [jax.ShapeDtypeStruct((d, 4 * width), BF16), jax.ShapeDtypeStruct((d, width), BF16)]
    else:
        weights = w_in
        gate_spec = pl.BlockSpec((d, width), lambda i, j: (0, 0))
        extra_specs, extra_shapes = [], []
    outs = pl.pallas_call(
        functools.partial(_mix_in_kernel, emit_bf16=emit_bf16),
        grid=grid,
        in_specs=[
            pl.BlockSpec((tm, d), lambda i, j: (i, 0)),
            pl.BlockSpec((1, d), lambda i, j: (0, 0)),
            main_spec,
            gate_spec,
        ],
        out_specs=[out_spec] * 4 + extra_specs,
        out_shape=[out_sds] * 4 + extra_shapes,
        scratch_shapes=[pltpu.VMEM((tm, d), BF16)],
        compiler_params=_params("arbitrary" if emit_bf16 else "parallel", "arbitrary"),
        name="mix_in_cast" if emit_bf16 else "mix_in",
    )(x, gain.reshape(1, d), *weights)
    if emit_bf16:
        return outs[:4], (outs[4], outs[5])
    return outs


def _diff_lambda(lam_ref, lam_init):
    lv = lam_ref[...]
    d1 = jnp.sum(lv[0:1] * lv[1:2], axis=-1, keepdims=True)
    d2 = jnp.sum(lv[2:3] * lv[3:4], axis=-1, keepdims=True)
    return jnp.exp(d1) - jnp.exp(d2) + lam_init


def _attn_prompt_kernel(slopes_ref, q_ref, k_ref, v_ref, lam_ref, sub_ref, o_ref,
                        k1_ref, k2_ref, vb_ref, dist_ref, sd_ref, m_ref, l_ref, acc_ref,
                        *, seq, tq, tk, hps, lam_init):
    head0 = pl.program_id(1) * hps
    seq_pad = vb_ref.shape[1]
    k_refs = (k1_ref, k2_ref)

    dist = (lax.broadcasted_iota(jnp.int32, (tq, tk), 1)
            - lax.broadcasted_iota(jnp.int32, (tq, tk), 0))
    dist_ref[...] = dist
    slope2 = []
    for hh in range(hps):
        cols = slice(hh * HEAD_WIDTH, (hh + 1) * HEAD_WIDTH)
        k = k_ref[0, :, cols]
        k1_ref[hh, 0:seq, :] = k[:, :HEAD_DIM].astype(BF16)
        k2_ref[hh, 0:seq, :] = k[:, HEAD_DIM:].astype(BF16)
        vb_ref[hh, 0:seq, :] = v_ref[0, :, cols].astype(BF16)
        if seq_pad > seq:
            k1_ref[hh, seq:seq_pad, :] = jnp.zeros((seq_pad - seq, HEAD_DIM), BF16)
            k2_ref[hh, seq:seq_pad, :] = jnp.zeros((seq_pad - seq, HEAD_DIM), BF16)
            vb_ref[hh, seq:seq_pad, :] = jnp.zeros((seq_pad - seq, HEAD_WIDTH), BF16)
        slope2.append(slopes_ref[head0 + hh] * LOG2E)
        sd_ref[hh] = slope2[hh] * dist.astype(F32)
    lam = _diff_lambda(lam_ref, lam_init)
    sub_gain = sub_ref[...] * (1.0 - lam_init)
    stat = (tq, LANES)

    def q_tile(i, carry):
        q0 = pl.multiple_of(i * tq, SUBLANES)
        qt = q_ref[0, pl.ds(q0, tq), :] * LOG2E
        qs = [qt[:, n * HEAD_DIM:(n + 1) * HEAD_DIM].astype(BF16) for n in range(2 * hps)]
        m_ref[...] = jnp.full(m_ref.shape, -jnp.inf, F32)
        l_ref[...] = jnp.zeros_like(l_ref)
        acc_ref[...] = jnp.zeros_like(acc_ref)

        def kv_tile(j, carry, masked):
            k0 = pl.multiple_of(j * tk, tk)
            off = q0 - k0
            for hh in range(hps):
                vt = vb_ref[hh, pl.ds(k0, tk), :]
                tile_bias = -slope2[hh] * off.astype(F32)
                for mp in range(2):
                    n = 2 * hh + mp
                    kmp = k_refs[mp][hh, pl.ds(k0, tk), :]
                    s = lax.dot_general(qs[n], kmp, _NT, preferred_element_type=F32) + sd_ref[hh]
                    if masked:
                        s = jnp.where(dist_ref[...] <= off, s, -jnp.inf)
                    m_old = m_ref[n]
                    m_tile = jnp.broadcast_to(jnp.max(s, axis=-1, keepdims=True), stat) + tile_bias
                    m_new = jnp.maximum(m_old, m_tile)
                    corr = jnp.exp2(m_old - m_new)
                    shift = tile_bias - m_new
                    wide = jnp.concatenate([shift] * -(-tk // LANES), axis=1)[:, :tk]
                    p = jnp.exp2(s + wide)
                    l_ref[n] = l_ref[n] * corr + jnp.broadcast_to(
                        jnp.sum(p, axis=-1, keepdims=True), stat)
                    acc_ref[n] = acc_ref[n] * corr + jnp.dot(
                        p.astype(BF16), vt, preferred_element_type=F32)
                    m_ref[n] = m_new
            return carry

        n_full = q0 // tk
        n_all = (q0 + tq + tk - 1) // tk
        lax.fori_loop(0, n_full, functools.partial(kv_tile, masked=False), 0)
        lax.fori_loop(n_full, n_all, functools.partial(kv_tile, masked=True), 0)

        for hh in range(hps):
            o1 = acc_ref[2 * hh] / l_ref[2 * hh]
            o2 = acc_ref[2 * hh + 1] / l_ref[2 * hh + 1]
            o_ref[0, pl.ds(q0, tq), hh * HEAD_WIDTH:(hh + 1) * HEAD_WIDTH] = _rms(o1 - lam * o2, sub_gain)
        return carry

    lax.fori_loop(0, seq // tq, q_tile, 0)


def _seq_tile(seq, target):
    best = None
    for t in range(SUBLANES, min(seq, target) + 1, SUBLANES):
        if seq % t == 0:
            best = t
    assert best is not None, (seq, target)
    return best


def _attn_prompt(q, k, v, lam_vecs, subln, slopes, lam_init):
    batch, seq, width = q.shape
    heads = width // HEAD_WIDTH
    assert HEAD_WIDTH == LANES
    hps = 2 if heads % 2 == 0 else 1
    tq = _row_tile(seq, 704)
    tk = tq
    seq_pad = -(-seq // tk) * tk
    blk = pl.BlockSpec((1, seq, hps * HEAD_WIDTH), lambda b, h: (b, 0, h))
    return pl.pallas_call(
        functools.partial(_attn_prompt_kernel, seq=seq, tq=tq, tk=tk, hps=hps, lam_init=lam_init),
        grid=(batch, heads // hps),
        in_specs=[
            pl.BlockSpec(memory_space=pltpu.SMEM),
            blk, blk, blk,
            pl.BlockSpec((4, HEAD_DIM), lambda b, h: (0, 0)),
            pl.BlockSpec((1, HEAD_WIDTH), lambda b, h: (0, 0)),
        ],
        out_specs=blk,
        out_shape=jax.ShapeDtypeStruct((batch, seq, width), F32),
        scratch_shapes=[
            pltpu.VMEM((hps, seq_pad, HEAD_DIM), BF16),
            pltpu.VMEM((hps, seq_pad, HEAD_DIM), BF16),
            pltpu.VMEM((hps, seq_pad, HEAD_WIDTH), BF16),
            pltpu.VMEM((tq, tk), jnp.int32),
            pltpu.VMEM((hps, tq, tk), F32),
            pltpu.VMEM((2 * hps, tq, LANES), F32),
            pltpu.VMEM((2 * hps, tq, LANES), F32),
            pltpu.VMEM((2 * hps, tq, HEAD_WIDTH), F32),
        ],
        compiler_params=_params("parallel", "parallel"),
        name="attn_prompt",
    )(slopes, q, k, v, lam_vecs, subln.reshape(1, HEAD_WIDTH))


class _Decode(NamedTuple):
    n_req: int
    n_chunks: int
    pages: int
    heads: int
    past: int
    lam_init: float
    value_parts: int
    n_inputs: int


def _decode_phases(c, cfg, dec_in, o_ref, scratch):
    pages, heads, past, lam_init = cfg.pages, cfg.heads, cfg.past, cfg.lam_init
    slopes_ref, q_ref, kn_ref, vn_ref, lam_ref, sub_ref = dec_in[:6]
    k_refs = dec_in[6:6 + pages]
    v_refs = dec_in[6 + pages:6 + 2 * pages]
    wq_ref, b0_ref, sl_ref, m_ref, l_ref, acc_ref, kc_ref, vc_ref = scratch
    n_tok = q_ref.shape[1]
    width, cols = wq_ref.shape
    group = 2 * n_tok
    page = k_refs[0].shape[1] // heads
    chunk = pages * page

    def head_group(h):
        return slice(h * group, (h + 1) * group)

    def head_cols(h):
        return slice(h * HEAD_WIDTH, (h + 1) * HEAD_WIDTH)

    def as_rows(stat):
        return jnp.broadcast_to(stat, (HEAD_WIDTH, cols)).T

    def start():
        o_ref[...] = jnp.zeros_like(o_ref)
        rep = jnp.concatenate([q_ref[0]] * (cols // n_tok), axis=0)
        r_io = lax.broadcasted_iota(jnp.int32, (cols, width), 0)
        c_io = lax.broadcasted_iota(jnp.int32, (cols, width), 1)
        keep = _idiv(c_io, HEAD_DIM) == _idiv(r_io, n_tok)
        wq_ref[...] = jnp.where(keep, rep, 0.0).T.astype(BF16)
        col = lax.broadcasted_iota(jnp.int32, (1, cols), 1)
        slope_row = jnp.zeros((1, cols), F32)
        for h in range(heads):
            slope_row = jnp.where(_idiv(col, group) == h, slopes_ref[h], slope_row)
        sl_ref[...] = slope_row
        key = lax.broadcasted_iota(jnp.int32, (chunk, cols), 0)
        tok = _imod(lax.broadcasted_iota(jnp.int32, (chunk, cols), 1), n_tok)
        b0_ref[...] = slope_row * (key - tok).astype(F32)
        m_ref[...] = jnp.full(m_ref.shape, -jnp.inf, F32)
        l_ref[...] = jnp.zeros_like(l_ref)
        acc_ref[...] = jnp.zeros_like(acc_ref)

    def softmax_part(s):
        m_old = m_ref[...]
        m_new = jnp.maximum(m_old, jnp.max(s, axis=0, keepdims=True))
        corr = jnp.exp(m_old - m_new)
        p = jnp.exp(s - m_new)
        l_ref[...] = l_ref[...] * corr + jnp.sum(p, axis=0, keepdims=True)
        m_ref[...] = m_new
        return p.T.astype(BF16), as_rows(corr)

    span = 2 if heads % 2 == 0 else 1

    def values_part(part, pb, corr_rows, n_keys):
        h0 = part * span
        z = jnp.dot(pb[h0 * group:(h0 + span) * group],
                    vc_ref[0:n_keys, h0 * HEAD_WIDTH:(h0 + span) * HEAD_WIDTH],
                    preferred_element_type=F32)
        for i in range(span):
            hg = head_group(h0 + i)
            acc_ref[hg, :] = acc_ref[hg, :] * corr_rows[hg] + z[i * group:(i + 1) * group,
                                                                i * HEAD_WIDTH:(i + 1) * HEAD_WIDTH]

    def scores(n_keys):
        return jnp.dot(kc_ref[0:n_keys, :], wq_ref[...], preferred_element_type=F32) + b0_ref[0:n_keys, :]

    def regroup(page_refs, dst_ref, head_range):
        for p in range(pages):
            for t in range(0, page, BF16_ROWS):
                for h in head_range:
                    src = pl.ds(t * heads + h, BF16_ROWS, stride=heads)
                    dst = slice(p * page + t, p * page + t + BF16_ROWS)
                    dst_ref[dst, head_cols(h)] = page_refs[p][0, src, :].astype(BF16)

    def chunk_scores():
        regroup(k_refs, kc_ref, range(heads))
        chunk_shift = sl_ref[...] * (c * chunk - past).astype(F32)
        return softmax_part(scores(chunk) + chunk_shift)

    def chunk_values(part, pb, corr_rows):
        regroup(v_refs, vc_ref, range(part * span, (part + 1) * span))
        values_part(part, pb, corr_rows, chunk)

    def finish():
        pad = jnp.zeros((page - n_tok, width), F32)
        kc_ref[0:page, :] = jnp.concatenate([kn_ref[0], pad], axis=0).astype(BF16)
        vc_ref[0:page, :] = jnp.concatenate([vn_ref[0], pad], axis=0).astype(BF16)
        key = lax.broadcasted_iota(jnp.int32, (page, cols), 0)
        tok = _imod(lax.broadcasted_iota(jnp.int32, (page, cols), 1), n_tok)
        new_state = softmax_part(jnp.where(key <= tok, scores(page), -jnp.inf))
        for part in range(heads // span):
            values_part(part, *new_state, page)

        lam = _diff_lambda(lam_ref, lam_init)
        sub_gain = sub_ref[...] * (1.0 - lam_init)
        l_rows = as_rows(l_ref[...])
        for h in range(heads):
            hg = head_group(h)
            blk = acc_ref[hg, :] / l_rows[hg]
            o_ref[0, :, head_cols(h)] = _rms(blk[0:n_tok] - lam * blk[n_tok:group], sub_gain)

    return start, chunk_scores, chunk_values, finish


def _decode_plan(grid, q, k_new, v_new, cache_k, cache_v, layer, page_table, lam_vecs, subln, slopes, lam_init):
    n_req, n_tok, width = q.shape
    n_pages = page_table.shape[1]
    depth, n_pool, page, heads, _ = cache_k.shape
    group = 2 * n_tok
    assert group % BF16_ROWS == 0 and page % BF16_ROWS == 0 and n_tok <= page
    cols = heads * group
    pages = 8
    while n_pages % pages:
        pages //= 2
    chunk = pages * page
    n_chunks = n_pages // pages
    gi, gj = grid
    assert gi * gj >= n_req * n_chunks, (grid, n_req, n_chunks)
    n_slots = -(-(gi * gj) // n_chunks)
    cache_k = cache_k.reshape(depth * n_pool, page * heads, HEAD_WIDTH)
    cache_v = cache_v.reshape(depth * n_pool, page * heads, HEAD_WIDTH)

    def slot(i, j):
        return (i * gj + j) // n_chunks

    def page_spec(p):
        def index(i, j, pt):
            step = i * gj + j
            req = jnp.minimum(step // n_chunks, n_req - 1)
            return (layer * n_pool + pt[req * n_pages + (step % n_chunks) * pages + p], 0, 0)
        return pl.BlockSpec((1, page * heads, HEAD_WIDTH), index)

    tok_spec = pl.BlockSpec((1, n_tok, width), lambda i, j, pt: (jnp.minimum(slot(i, j), n_req - 1), 0, 0))
    specs = [
        pl.BlockSpec(memory_space=pltpu.SMEM),
        tok_spec, tok_spec, tok_spec,
        pl.BlockSpec((4, HEAD_DIM), lambda i, j, pt: (0, 0)),
        pl.BlockSpec((1, HEAD_WIDTH), lambda i, j, pt: (0, 0)),
    ] + [page_spec(p) for p in range(pages)] * 2
    args = [slopes, q, k_new, v_new, lam_vecs, subln.reshape(1, HEAD_WIDTH)] + [cache_k] * pages + [cache_v] * pages
    att_spec = pl.BlockSpec((1, n_tok, width), lambda i, j, pt: (slot(i, j), 0, 0))
    att_shape = jax.ShapeDtypeStruct((n_slots, n_tok, width), F32)
    scratch = [
        pltpu.VMEM((width, cols), BF16),
        pltpu.VMEM((chunk, cols), F32),
        pltpu.VMEM((1, cols), F32),
        pltpu.VMEM((1, cols), F32),
        pltpu.VMEM((1, cols), F32),
        pltpu.VMEM((cols, HEAD_WIDTH), F32),
        pltpu.VMEM((chunk, width), BF16),
        pltpu.VMEM((chunk, width), BF16),
    ]
    cfg = _Decode(n_req=n_req, n_chunks=n_chunks, pages=pages, heads=heads, past=n_pages * page,
                  lam_init=lam_init, value_parts=heads // (2 if heads % 2 == 0 else 1), n_inputs=len(specs))
    return cfg, page_table.reshape(-1), specs, args, att_spec, att_shape, scratch


def _conv_kernel(*refs, taps, rc, halo, has_hist):
    if has_hist:
        x_ref, h_ref, w_ref, cb_ref, g_ref, b_ref, o_ref, wb_ref = refs
    else:
        x_ref, w_ref, cb_ref, g_ref, b_ref, o_ref, wb_ref = refs
    seq, ch = x_ref.shape[1], x_ref.shape[2]
    lead = halo - (taps - 1)
    n_out = rc // SUBLANES
    n_win = n_out + halo // SUBLANES
    n_lb = ch // LANES

    for j in range(taps):
        wb_ref[j] = jnp.broadcast_to(w_ref[j:j + 1, :], (SUBLANES, ch))
    sublane = lax.broadcasted_iota(jnp.int32, (SUBLANES, LANES), 0)

    def conv_rows(load_tile, r0):
        parts = [[None] * n_lb for _ in range(n_out)]
        for lb in range(n_lb):
            ls = slice(lb * LANES, (lb + 1) * LANES)
            xv = [load_tile(i, ls) for i in range(n_win)]
            out = [None] * n_out
            for s in range(SUBLANES):
                a_vals = [a for a in range(n_win) if lead <= SUBLANES * a + s <= lead + taps - 1]
                if not a_vals:
                    continue
                part = [None] * (n_out + (1 if s else 0))
                for a in a_vals:
                    w = wb_ref[SUBLANES * a + s - lead, :, ls]
                    for m in range(len(part)):
                        term = xv[m + a] * w
                        part[m] = term if part[m] is None else part[m] + term
                for m in range(n_out):
                    if s == 0:
                        u = part[m]
                    else:
                        u = pltpu.roll(jnp.where(sublane >= s, part[m], part[m + 1]), SUBLANES - s, 0)
                    out[m] = u if out[m] is None else out[m] + u
            for m in range(n_out):
                parts[m][lb] = out[m]
        rows = []
        for m in range(n_out):
            y = jnp.concatenate(parts[m], axis=1) + cb_ref[...]
            mu = jnp.mean(y, axis=-1, keepdims=True)
            yc = y - mu
            var = jnp.mean(yc * yc, axis=-1, keepdims=True)
            z = yc * lax.rsqrt(var + EPS) * g_ref[...] + b_ref[...]
            rows.append(z * jax.nn.sigmoid(z))
        o_ref[0, pl.ds(r0, rc), :] = jnp.concatenate(rows, axis=0).astype(o_ref.dtype)

    n_static = min(-(-halo // rc), seq // rc)
    for r in range(n_static):
        def static_tile(i, ls, r=r):
            row = r * rc - halo + i * SUBLANES
            if row >= 0:
                return x_ref[0, row:row + SUBLANES, ls]
            if has_hist:
                return h_ref[0, halo + row:halo + row + SUBLANES, ls]
            return jnp.zeros((SUBLANES, LANES), F32)

        conv_rows(static_tile, r * rc)

    def chunk(r, carry):
        r0 = pl.multiple_of(r * rc, SUBLANES)

        def tile(i, ls):
            return x_ref[0, pl.ds(pl.multiple_of(r0 - halo + i * SUBLANES, SUBLANES), SUBLANES), ls]

        conv_rows(tile, r0)
        return carry

    if seq // rc > n_static:
        lax.fori_loop(n_static, seq // rc, chunk, 0)


def _conv_group(glu, state, conv_w, conv_b, norm_g, norm_b):
    batch, seq, ch = glu.shape
    taps = conv_w.shape[0]
    halo = -(-(taps - 1) // SUBLANES) * SUBLANES
    rc = _seq_tile(seq, 48)
    has_hist = state is not None
    out_dtype = BF16 if rc % BF16_ROWS == 0 else F32
    vec = pl.BlockSpec((1, ch), lambda b: (0, 0))
    in_specs = [pl.BlockSpec((1, seq, ch), lambda b: (b, 0, 0))]
    args = [glu]
    if has_hist:
        in_specs.append(pl.BlockSpec((1, halo, ch), lambda b: (b, 0, 0)))
        args.append(jnp.pad(state, ((0, 0), (halo - (taps - 1), 0), (0, 0))))
    in_specs += [pl.BlockSpec((taps, ch), lambda b: (0, 0)), vec, vec, vec]
    args += [conv_w, conv_b.reshape(1, ch), norm_g.reshape(1, ch), norm_b.reshape(1, ch)]
    return pl.pallas_call(
        functools.partial(_conv_kernel, taps=taps, rc=rc, halo=halo, has_hist=has_hist),
        grid=(batch,),
        in_specs=in_specs,
        out_specs=pl.BlockSpec((1, seq, ch), lambda b: (b, 0, 0)),
        out_shape=jax.ShapeDtypeStruct((batch, seq, ch), out_dtype),
        scratch_shapes=[pltpu.VMEM((taps, SUBLANES, ch), F32)],
        compiler_params=_params("parallel"),
        name="conv_group",
    )(*args)


def _mix_out_kernel(x_ref, a_ref, c_ref, wa_ref, wc_ref, o_ref):
    y = jnp.dot(a_ref[...].astype(BF16), wa_ref[...], preferred_element_type=F32)
    y += jnp.dot(c_ref[...].astype(BF16), wc_ref[...], preferred_element_type=F32)
    o_ref[...] = x_ref[...] + y


def _mix_out(x, att, conv, w_out):
    rows, d = x.shape
    wa = att.shape[1]
    wc = conv.shape[1]
    assert wa == wc and w_out.shape[0] == wa + wc
    tm = _row_tile(rows, 704)
    return pl.pallas_call(
        _mix_out_kernel,
        grid=(rows // tm,),
        in_specs=[
            pl.BlockSpec((tm, d), lambda i: (i, 0)),
            pl.BlockSpec((tm, wa), lambda i: (i, 0)),
            pl.BlockSpec((tm, wc), lambda i: (i, 0)),
            pl.BlockSpec((wa, d), lambda i: (0, 0)),
            pl.BlockSpec((wc, d), lambda i: (1, 0)),
        ],
        out_specs=pl.BlockSpec((tm, d), lambda i: (i, 0)),
        out_shape=jax.ShapeDtypeStruct((rows, d), F32),
        compiler_params=_params("parallel"),
        name="mix_out",
    )(x, att, conv, w_out, w_out)


def kernel(x_prompt, x_sample, cache_k, cache_v, state_conv, page_table, meta_tokens, ffn1_norm, ffn1_w_gate, ffn1_w_up, ffn1_w_down, mix_norm, w_in, lambda_q1, lambda_k1, lambda_q2, lambda_k2, attn_subln, conv_w, conv_b, conv_norm_g, conv_norm_b, w_out, ffn2_norm, ffn2_w_gate, ffn2_w_up, ffn2_w_down, final_norm):
    depth = w_in.shape[0]
    batch, seq0, d = x_prompt.shape
    n_meta = meta_tokens.shape[0]
    seq = seq0 + n_meta
    n_req, n_tok, _ = x_sample.shape
    taps = conv_w.shape[1]
    width = w_out.shape[1] // 2
    heads = width // HEAD_WIDTH
    slopes = jnp.array([2.0 ** (-8.0 * (i + 1) / heads) for i in range(heads)], F32)

    meta = jnp.broadcast_to(meta_tokens.astype(x_prompt.dtype)[None], (batch, n_meta, d))
    xp = jnp.concatenate([meta, x_prompt], axis=1).reshape(batch * seq, d)
    xs = x_sample.reshape(n_req * n_tok, d)

    outs = [[] for _ in range(6)]
    for l in range(depth):
        lam_init = 0.8 - 0.6 * math.exp(-0.3 * l)
        lam_vecs = jnp.stack([lambda_q1[l], lambda_k1[l], lambda_q2[l], lambda_k2[l]])
        wg2, wu2, wd2 = (w[l].astype(BF16) for w in (ffn2_w_gate, ffn2_w_up, ffn2_w_down))
        w_out_l = w_out[l].astype(BF16)
        last = l == depth - 1

        xs, wg1, wu1, wd1 = _ffn(xs, ffn1_norm[l], ffn1_w_gate[l], ffn1_w_up[l], ffn1_w_down[l])
        (qs, ks, vs, glus), w_in_l = _mix_in(xs, mix_norm[l], w_in[l], width)
        shape_s = (n_req, n_tok, width)
        qs, ks3, vs3, glus = (a.reshape(shape_s) for a in (qs, ks, vs, glus))
        half = -(-n_req // 2)

        def decode_args(lo, hi):
            return dict(q=qs[lo:hi], k_new=ks3[lo:hi], v_new=vs3[lo:hi], cache_k=cache_k, cache_v=cache_v,
                        layer=l, page_table=page_table[lo:hi], lam_vecs=lam_vecs, subln=attn_subln[l],
                        slopes=slopes, lam_init=lam_init)

        xp, att_s0 = _ffn(xp, ffn1_norm[l], wg1, wu1, wd1, decode_args=decode_args(0, half))

        qp, kp, vp, glup = _mix_in(xp, mix_norm[l], w_in_l, width)
        shape_p = (batch, seq, width)
        att_p = _attn_prompt(qp.reshape(shape_p), kp.reshape(shape_p), vp.reshape(shape_p),
                             lam_vecs, attn_subln[l], slopes, lam_init)
        glup = glup.reshape(shape_p)
        conv_p = _conv_group(glup, None, conv_w[l], conv_b[l], conv_norm_g[l], conv_norm_b[l])
        xp = _mix_out(xp, att_p.reshape(batch * seq, width), conv_p.reshape(batch * seq, width), w_out_l)
        outs[0].append(kp.reshape(batch, seq, heads, HEAD_WIDTH))
        outs[1].append(vp.reshape(batch, seq, heads, HEAD_WIDTH))
        outs[2].append(glup[:, seq - (taps - 1):])

        fg = final_norm if last else None
        xp, att_s1 = _ffn(xp, ffn2_norm[l], wg2, wu2, wd2, fg, decode_args=decode_args(half, n_req))

        att_s = jnp.concatenate([att_s0, att_s1], axis=0)
        conv_s = _conv_group(glus, state_conv[l], conv_w[l], conv_b[l], conv_norm_g[l], conv_norm_b[l])
        xs = _mix_out(xs, att_s.reshape(n_req * n_tok, width), conv_s.reshape(n_req * n_tok, width), w_out_l)
        outs[3].append(ks.reshape(n_req, n_tok, heads, HEAD_WIDTH))
        outs[4].append(vs.reshape(n_req, n_tok, heads, HEAD_WIDTH))
        outs[5].append(jnp.concatenate([state_conv[l], glus], axis=1)[:, -(taps - 1):])
        xs = _ffn(xs, ffn2_norm[l], wg2, wu2, wd2, fg)

    y_prompt = xp.reshape(batch, seq, d)[:, n_meta:]
    y_sample = xs.reshape(n_req, n_tok, d)
    return (y_prompt, y_sample) + tuple(jnp.stack(o) for o in outs)
```

```python
import functools
import math
from typing import NamedTuple

import jax
import jax.numpy as jnp
from jax import lax
from jax.experimental import pallas as pl
from jax.experimental.pallas import tpu as pltpu

EPS = 1e-6
HEAD_DIM = 64
HEAD_WIDTH = 2 * HEAD_DIM
LANES = 128
SUBLANES = 8
BF16_ROWS = 16
LOG2E = math.log2(math.e)
VMEM_LIMIT_BYTES = 56 * 1024 * 1024
VMEM_LIMIT_FUSED_BYTES = 61 * 1024 * 1024

BF16 = jnp.bfloat16
F32 = jnp.float32
_NT = (((1,), (1,)), ((), ()))


def _row_tile(rows, target):
    best = None
    for t in range(BF16_ROWS, min(rows, target) + 1, BF16_ROWS):
        if rows % t == 0:
            best = t
    assert best is not None, (rows, target)
    return best


def _col_tile(cols, target):
    best = None
    for t in range(LANES, min(cols, target) + 1, LANES):
        if cols % t == 0:
            best = t
    assert best is not None, (cols, target)
    return best


def _idiv(x, n):
    if n & (n - 1) == 0:
        return lax.shift_right_logical(x, n.bit_length() - 1)
    return x // n


def _imod(x, n):
    if n & (n - 1) == 0:
        return lax.bitwise_and(x, n - 1)
    return lax.rem(x, n)


def _rms(x, gain):
    return x * lax.rsqrt(jnp.mean(x * x, axis=-1, keepdims=True) + EPS) * gain


def _params(*semantics, vmem_limit_bytes=VMEM_LIMIT_BYTES):
    return pltpu.CompilerParams(dimension_semantics=semantics, vmem_limit_bytes=vmem_limit_bytes)


def _ffn_kernel(*refs, final, decode, emit_bf16):
    if decode is not None:
        refs = refs[1:]
    x_ref, g_ref, wg_ref, wu_ref, wd_ref = refs[:5]
    pos = 5
    if decode is not None:
        dec_in = refs[pos:pos + decode.n_inputs]
        pos += decode.n_inputs
    if final:
        fg_ref = refs[pos]
        pos += 1
    o_ref = refs[pos]
    pos += 1
    if decode is not None:
        att_ref = refs[pos]
        pos += 1
    if emit_bf16:
        wgb_ref, wub_ref, wdb_ref = refs[pos:pos + 3]
        pos += 3
    h_ref = refs[pos]
    dec_scratch = refs[pos + 1:]
    j = pl.program_id(1)

    @pl.when(j == 0)
    def _():
        h_ref[...] = _rms(x_ref[...], g_ref[...]).astype(BF16)
        o_ref[...] = jnp.zeros_like(o_ref)

    if decode is not None:
        step = pl.program_id(0) * pl.num_programs(1) + j
        c = lax.rem(step, decode.n_chunks)
        dec_start, dec_scores, dec_values, dec_finish = _decode_phases(c, decode, dec_in, att_ref, dec_scratch)
        pl.when(c == 0)(dec_start)

    h = h_ref[...]
    wg, wu, wd = wg_ref[...], wu_ref[...], wd_ref[...]
    if emit_bf16:
        wg, wu, wd = wg.astype(BF16), wu.astype(BF16), wd.astype(BF16)
        wgb_ref[...], wub_ref[...], wdb_ref[...] = wg, wu, wd
    gate = jnp.dot(h, wg, preferred_element_type=F32)
    if decode is not None:
        dec_state = dec_scores()
    up = jnp.dot(h, wu, preferred_element_type=F32)
    if decode is not None:
        for part in range(decode.value_parts):
            dec_values(part, *dec_state)
    act = (gate * jax.nn.sigmoid(gate) * up).astype(BF16)
    o_ref[...] += jnp.dot(act, wd, preferred_element_type=F32)

    if decode is not None:
        pl.when(c == decode.n_chunks - 1)(dec_finish)

    @pl.when(j == pl.num_programs(1) - 1)
    def _():
        y = x_ref[...] + 0.5 * o_ref[...]
        if final:
            y = _rms(y, fg_ref[...])
        o_ref[...] = y


def _ffn(x, gain, w_gate, w_up, w_down, final_gain=None, decode_args=None):
    rows, d = x.shape
    f = w_gate.shape[1]
    tm = _row_tile(rows, 704)
    tf = _col_tile(f, 512 if decode_args is None else 256)
    grid = (rows // tm, f // tf)
    final = final_gain is not None
    in_specs = [
        pl.BlockSpec((tm, d), lambda i, j, *_: (i, 0)),
        pl.BlockSpec((1, d), lambda i, j, *_: (0, 0)),
        pl.BlockSpec((d, tf), lambda i, j, *_: (0, j)),
        pl.BlockSpec((d, tf), lambda i, j, *_: (0, j)),
        pl.BlockSpec((tf, d), lambda i, j, *_: (j, 0)),
    ]
    args = [x, gain.reshape(1, d), w_gate, w_up, w_down]
    final_spec = [pl.BlockSpec((1, d), lambda i, j, *_: (0, 0))] if final else []
    final_arg = [final_gain.reshape(1, d)] if final else []
    out_specs = pl.BlockSpec((tm, d), lambda i, j, *_: (i, 0))
    out_shape = jax.ShapeDtypeStruct((rows, d), F32)
    scratch = [pltpu.VMEM((tm, d), BF16)]
    name = "half_ffn_final" if final else "half_ffn"
    emit_bf16 = w_gate.dtype != BF16
    if emit_bf16:
        assert decode_args is None and grid[0] == 1
        return pl.pallas_call(
            functools.partial(_ffn_kernel, final=final, decode=None, emit_bf16=True),
            grid=grid, in_specs=in_specs + final_spec,
            out_specs=[out_specs] + in_specs[2:5],
            out_shape=[out_shape] + [jax.ShapeDtypeStruct(w.shape, BF16) for w in (w_gate, w_up, w_down)],
            scratch_shapes=scratch, compiler_params=_params("arbitrary", "arbitrary"), name=name + "_cast",
        )(*args, *final_arg)
    if decode_args is None:
        return pl.pallas_call(
            functools.partial(_ffn_kernel, final=final, decode=None, emit_bf16=False),
            grid=grid, in_specs=in_specs + final_spec, out_specs=out_specs, out_shape=out_shape,
            scratch_shapes=scratch, compiler_params=_params("parallel", "arbitrary"), name=name,
        )(*args, *final_arg)

    decode, page_table, dec_specs, dec_args, att_spec, att_shape, dec_scratch = _decode_plan(
        grid, **decode_args)
    grid_spec = pltpu.PrefetchScalarGridSpec(
        num_scalar_prefetch=1, grid=grid,
        in_specs=in_specs + dec_specs + final_spec,
        out_specs=[out_specs, att_spec],
        scratch_shapes=scratch + dec_scratch,
    )
    y, att = pl.pallas_call(
        functools.partial(_ffn_kernel, final=final, decode=decode, emit_bf16=False),
        grid_spec=grid_spec,
        out_shape=[out_shape, att_shape],
        compiler_params=_params("arbitrary", "arbitrary", vmem_limit_bytes=VMEM_LIMIT_FUSED_BYTES),
        name=name + "_decode",
    )(page_table, *args, *dec_args, *final_arg)
    return y, att[:decode.n_req]


def _mix_in_kernel(x_ref, g_ref, wa_ref, wg_ref, q_ref, k_ref, v_ref, glu_ref, *rest, emit_bf16):
    if emit_bf16:
        wab_ref, wgb_ref, h_ref = rest
    else:
        (h_ref,) = rest
    j = pl.program_id(1)

    @pl.when(j == 0)
    def _():
        h_ref[...] = _rms(x_ref[...], g_ref[...]).astype(BF16)

    h = h_ref[...]
    wa = wa_ref[...]
    if emit_bf16:
        wa = wa.astype(BF16)
        wab_ref[...] = wa
    r = jnp.dot(h, wa, preferred_element_type=F32)

    @pl.when(j == 0)
    def _():
        q_ref[...] = r * (HEAD_DIM ** -0.5)

    @pl.when(j == 1)
    def _():
        k_ref[...] = r

    @pl.when(j == 2)
    def _():
        v_ref[...] = r

    @pl.when(j == 3)
    def _():
        wg = wg_ref[...]
        if emit_bf16:
            wg = wg.astype(BF16)
            wgb_ref[...] = wg
        gate = jnp.dot(h, wg, preferred_element_type=F32)
        glu_ref[...] = r * jax.nn.sigmoid(gate)


def _mix_in(x, gain, w_in, width):
    rows, d = x.shape
    tm = _row_tile(rows, 704)
    grid = (rows // tm, 4)
    out_spec = pl.BlockSpec((tm, width), lambda i, j: (i, 0))
    out_sds = jax.ShapeDtypeStruct((rows, width), F32)
    main_spec = pl.BlockSpec((d, width), lambda i, j: (0, j))
    emit_bf16 = not isinstance(w_in, tuple)
    if emit_bf16:
        assert w_in.shape[1] == 5 * width and grid[0] == 1
        weights = (w_in, w_in)
        gate_spec = pl.BlockSpec((d, width), lambda i, j: (0, 4))
        extra_specs = [main_spec, pl.BlockSpec((d, width), lambda i, j: (0, 0))]
        extra_shapes = [jax.ShapeDtypeStruct((d, 4 * width), BF16), jax.ShapeDtypeStruct((d, width), BF16)]
    else:
        weights = w_in
        gate_spec = pl.BlockSpec((d, width), lambda i, j: (0, 0))
        extra_specs, extra_shapes = [], []
    outs = pl.pallas_call(
        functools.partial(_mix_in_kernel, emit_bf16=emit_bf16),
        grid=grid,
        in_specs=[
            pl.BlockSpec((tm, d), lambda i, j: (i, 0)),
            pl.BlockSpec((1, d), lambda i, j: (0, 0)),
            main_spec,
            gate_spec,
        ],
        out_specs=[out_spec] * 4 + extra_specs,
        out_shape=[out_sds] * 4 + extra_shapes,
        scratch_shapes=[pltpu.VMEM((tm, d), BF16)],
        compiler_params=_params("arbitrary" if emit_bf16 else "parallel", "arbitrary"),
        name="mix_in_cast" if emit_bf16 else "mix_in",
    )(x, gain.reshape(1, d), *weights)
    if emit_bf16:
        return outs[:4], (outs[4], outs[5])
    return outs


def _diff_lambda(lam_ref, lam_init):
    lv = lam_ref[...]
    d1 = jnp.sum(lv[0:1] * lv[1:2], axis=-1, keepdims=True)
    d2 = jnp.sum(lv[2:3] * lv[3:4], axis=-1, keepdims=True)
    return jnp.exp(d1) - jnp.exp(d2) + lam_init


def _attn_prompt_kernel(slopes_ref, q_ref, k_ref, v_ref, lam_ref, sub_ref, o_ref,
                        k1_ref, k2_ref, vb_ref, dist_ref, sd_ref, m_ref, l_ref, acc_ref,
                        *, seq, tq, tk, hps, lam_init):
    head0 = pl.program_id(1) * hps
    seq_pad = vb_ref.shape[1]
    k_refs = (k1_ref, k2_ref)

    dist = (lax.broadcasted_iota(jnp.int32, (tq, tk), 1)
            - lax.broadcasted_iota(jnp.int32, (tq, tk), 0))
    dist_ref[...] = dist
    slope2 = []
    for hh in range(hps):
        cols = slice(hh * HEAD_WIDTH, (hh + 1) * HEAD_WIDTH)
        k = k_ref[0, :, cols]
        k1_ref[hh, 0:seq, :] = k[:, :HEAD_DIM].astype(BF16)
        k2_ref[hh, 0:seq, :] = k[:, HEAD_DIM:].astype(BF16)
        vb_ref[hh, 0:seq, :] = v_ref[0, :, cols].astype(BF16)
        if seq_pad > seq:
            k1_ref[hh, seq:seq_pad, :] = jnp.zeros((seq_pad - seq, HEAD_DIM), BF16)
            k2_ref[hh, seq:seq_pad, :] = jnp.zeros((seq_pad - seq, HEAD_DIM), BF16)
            vb_ref[hh, seq:seq_pad, :] = jnp.zeros((seq_pad - seq, HEAD_WIDTH), BF16)
        slope2.append(slopes_ref[head0 + hh] * LOG2E)
        sd_ref[hh] = slope2[hh] * dist.astype(F32)
    lam = _diff_lambda(lam_ref, lam_init)
    sub_gain = sub_ref[...] * (1.0 - lam_init)
    stat = (tq, LANES)

    def q_tile(i, carry):
        q0 = pl.multiple_of(i * tq, BF16_ROWS)
        qt = q_ref[0, pl.ds(q0, tq), :] * LOG2E
        qs = [qt[:, n * HEAD_DIM:(n + 1) * HEAD_DIM].astype(BF16) for n in range(2 * hps)]
        m_ref[...] = jnp.full(m_ref.shape, -jnp.inf, F32)
        l_ref[...] = jnp.zeros_like(l_ref)
        acc_ref[...] = jnp.zeros_like(acc_ref)

        def kv_tile(j, carry, masked):
            k0 = pl.multiple_of(j * tk, tk)
            off = q0 - k0
            for hh in range(hps):
                vt = vb_ref[hh, pl.ds(k0, tk), :]
                tile_bias = -slope2[hh] * off.astype(F32)
                for mp in range(2):
                    n = 2 * hh + mp
                    kmp = k_refs[mp][hh, pl.ds(k0, tk), :]
                    s = lax.dot_general(qs[n], kmp, _NT, preferred_element_type=F32) + sd_ref[hh]
                    if masked:
                        s = jnp.where(dist_ref[...] <= off, s, -jnp.inf)
                    m_old = m_ref[n]
                    m_tile = jnp.broadcast_to(jnp.max(s, axis=-1, keepdims=True), stat) + tile_bias
                    m_new = jnp.maximum(m_old, m_tile)
                    corr = jnp.exp2(m_old - m_new)
                    shift = tile_bias - m_new
                    wide = jnp.concatenate([shift] * -(-tk // LANES), axis=1)[:, :tk]
                    p = jnp.exp2(s + wide)
                    l_ref[n] = l_ref[n] * corr + jnp.broadcast_to(
                        jnp.sum(p, axis=-1, keepdims=True), stat)
                    acc_ref[n] = acc_ref[n] * corr + jnp.dot(
                        p.astype(BF16), vt, preferred_element_type=F32)
                    m_ref[n] = m_new
            return carry

        n_full = q0 // tk
        n_all = (q0 + tq + tk - 1) // tk
        lax.fori_loop(0, n_full, functools.partial(kv_tile, masked=False), 0)
        lax.fori_loop(n_full, n_all, functools.partial(kv_tile, masked=True), 0)

        for hh in range(hps):
            o1 = acc_ref[2 * hh] / l_ref[2 * hh]
            o2 = acc_ref[2 * hh + 1] / l_ref[2 * hh + 1]
            o_ref[0, pl.ds(q0, tq), hh * HEAD_WIDTH:(hh + 1) * HEAD_WIDTH] = _rms(
                o1 - lam * o2, sub_gain).astype(o_ref.dtype)
        return carry

    lax.fori_loop(0, seq // tq, q_tile, 0)


def _seq_tile(seq, target):
    best = None
    for t in range(SUBLANES, min(seq, target) + 1, SUBLANES):
        if seq % t == 0:
            best = t
    assert best is not None, (seq, target)
    return best


def _attn_prompt(q, k, v, lam_vecs, subln, slopes, lam_init):
    batch, seq, width = q.shape
    heads = width // HEAD_WIDTH
    assert HEAD_WIDTH == LANES
    hps = 2 if heads % 2 == 0 else 1
    tq = _row_tile(seq, 704)
    tk = tq
    seq_pad = -(-seq // tk) * tk
    blk = pl.BlockSpec((1, seq, hps * HEAD_WIDTH), lambda b, h: (b, 0, h))
    return pl.pallas_call(
        functools.partial(_attn_prompt_kernel, seq=seq, tq=tq, tk=tk, hps=hps, lam_init=lam_init),
        grid=(batch, heads // hps),
        in_specs=[
            pl.BlockSpec(memory_space=pltpu.SMEM),
            blk, blk, blk,
            pl.BlockSpec((4, HEAD_DIM), lambda b, h: (0, 0)),
            pl.BlockSpec((1, HEAD_WIDTH), lambda b, h: (0, 0)),
        ],
        out_specs=blk,
        out_shape=jax.ShapeDtypeStruct((batch, seq, width), BF16),
        scratch_shapes=[
            pltpu.VMEM((hps, seq_pad, HEAD_DIM), BF16),
            pltpu.VMEM((hps, seq_pad, HEAD_DIM), BF16),
            pltpu.VMEM((hps, seq_pad, HEAD_WIDTH), BF16),
            pltpu.VMEM((tq, tk), jnp.int32),
            pltpu.VMEM((hps, tq, tk), F32),
            pltpu.VMEM((2 * hps, tq, LANES), F32),
            pltpu.VMEM((2 * hps, tq, LANES), F32),
            pltpu.VMEM((2 * hps, tq, HEAD_WIDTH), F32),
        ],
        compiler_params=_params("parallel", "parallel"),
        name="attn_prompt",
    )(slopes, q, k, v, lam_vecs, subln.reshape(1, HEAD_WIDTH))


class _Decode(NamedTuple):
    n_req: int
    n_chunks: int
    pages: int
    heads: int
    past: int
    lam_init: float
    value_parts: int
    n_inputs: int


def _decode_phases(c, cfg, dec_in, o_ref, scratch):
    pages, heads, past, lam_init = cfg.pages, cfg.heads, cfg.past, cfg.lam_init
    slopes_ref, q_ref, kn_ref, vn_ref, lam_ref, sub_ref = dec_in[:6]
    k_refs = dec_in[6:6 + pages]
    v_refs = dec_in[6 + pages:6 + 2 * pages]
    wq_ref, b0_ref, sl_ref, m_ref, l_ref, acc_ref, kc_ref, vc_ref = scratch
    n_tok = q_ref.shape[1]
    width, cols = wq_ref.shape
    group = 2 * n_tok
    page = k_refs[0].shape[1] // heads
    chunk = pages * page

    def head_group(h):
        return slice(h * group, (h + 1) * group)

    def head_cols(h):
        return slice(h * HEAD_WIDTH, (h + 1) * HEAD_WIDTH)

    def as_rows(stat):
        return jnp.broadcast_to(stat, (HEAD_WIDTH, cols)).T

    def start():
        o_ref[...] = jnp.zeros_like(o_ref)
        rep = jnp.concatenate([q_ref[0]] * (cols // n_tok), axis=0)
        r_io = lax.broadcasted_iota(jnp.int32, (cols, width), 0)
        c_io = lax.broadcasted_iota(jnp.int32, (cols, width), 1)
        keep = _idiv(c_io, HEAD_DIM) == _idiv(r_io, n_tok)
        wq_ref[...] = jnp.where(keep, rep, 0.0).T.astype(BF16)
        col = lax.broadcasted_iota(jnp.int32, (1, cols), 1)
        slope_row = jnp.zeros((1, cols), F32)
        for h in range(heads):
            slope_row = jnp.where(_idiv(col, group) == h, slopes_ref[h], slope_row)
        sl_ref[...] = slope_row
        key = lax.broadcasted_iota(jnp.int32, (chunk, cols), 0)
        tok = _imod(lax.broadcasted_iota(jnp.int32, (chunk, cols), 1), n_tok)
        b0_ref[...] = slope_row * (key - tok).astype(F32)
        m_ref[...] = jnp.full(m_ref.shape, -jnp.inf, F32)
        l_ref[...] = jnp.zeros_like(l_ref)
        acc_ref[...] = jnp.zeros_like(acc_ref)

    def softmax_part(s):
        m_old = m_ref[...]
        m_new = jnp.maximum(m_old, jnp.max(s, axis=0, keepdims=True))
        corr = jnp.exp(m_old - m_new)
        p = jnp.exp(s - m_new)
        l_ref[...] = l_ref[...] * corr + jnp.sum(p, axis=0, keepdims=True)
        m_ref[...] = m_new
        return p.T.astype(BF16), as_rows(corr)

    span = 2 if heads % 2 == 0 else 1

    def values_part(part, pb, corr_rows, n_keys):
        h0 = part * span
        z = jnp.dot(pb[h0 * group:(h0 + span) * group],
                    vc_ref[0:n_keys, h0 * HEAD_WIDTH:(h0 + span) * HEAD_WIDTH],
                    preferred_element_type=F32)
        for i in range(span):
            hg = head_group(h0 + i)
            acc_ref[hg, :] = acc_ref[hg, :] * corr_rows[hg] + z[i * group:(i + 1) * group,
                                                                i * HEAD_WIDTH:(i + 1) * HEAD_WIDTH]

    def scores(n_keys):
        return jnp.dot(kc_ref[0:n_keys, :], wq_ref[...], preferred_element_type=F32) + b0_ref[0:n_keys, :]

    def regroup(page_refs, dst_ref, head_range):
        for p in range(pages):
            for t in range(0, page, BF16_ROWS):
                for h in head_range:
                    src = pl.ds(t * heads + h, BF16_ROWS, stride=heads)
                    dst = slice(p * page + t, p * page + t + BF16_ROWS)
                    dst_ref[dst, head_cols(h)] = page_refs[p][0, src, :].astype(BF16)

    def chunk_scores():
        regroup(k_refs, kc_ref, range(heads))
        chunk_shift = sl_ref[...] * (c * chunk - past).astype(F32)
        return softmax_part(scores(chunk) + chunk_shift)

    def chunk_values(part, pb, corr_rows):
        regroup(v_refs, vc_ref, range(part * span, (part + 1) * span))
        values_part(part, pb, corr_rows, chunk)

    def finish():
        pad = jnp.zeros((page - n_tok, width), F32)
        kc_ref[0:page, :] = jnp.concatenate([kn_ref[0], pad], axis=0).astype(BF16)
        vc_ref[0:page, :] = jnp.concatenate([vn_ref[0], pad], axis=0).astype(BF16)
        key = lax.broadcasted_iota(jnp.int32, (page, cols), 0)
        tok = _imod(lax.broadcasted_iota(jnp.int32, (page, cols), 1), n_tok)
        new_state = softmax_part(jnp.where(key <= tok, scores(page), -jnp.inf))
        for part in range(heads // span):
            values_part(part, *new_state, page)

        lam = _diff_lambda(lam_ref, lam_init)
        sub_gain = sub_ref[...] * (1.0 - lam_init)
        l_rows = as_rows(l_ref[...])
        for h in range(heads):
            hg = head_group(h)
            blk = acc_ref[hg, :] / l_rows[hg]
            o_ref[0, :, head_cols(h)] = _rms(blk[0:n_tok] - lam * blk[n_tok:group], sub_gain)

    return start, chunk_scores, chunk_values, finish


def _decode_plan(grid, q, k_new, v_new, cache_k, cache_v, layer, page_table, lam_vecs, subln, slopes, lam_init):
    n_req, n_tok, width = q.shape
    n_pages = page_table.shape[1]
    depth, n_pool, page, heads, _ = cache_k.shape
    group = 2 * n_tok
    assert group % BF16_ROWS == 0 and page % BF16_ROWS == 0 and n_tok <= page
    cols = heads * group
    pages = 8
    while n_pages % pages:
        pages //= 2
    chunk = pages * page
    n_chunks = n_pages // pages
    gi, gj = grid
    assert gi * gj >= n_req * n_chunks, (grid, n_req, n_chunks)
    n_slots = -(-(gi * gj) // n_chunks)
    cache_k = cache_k.reshape(depth * n_pool, page * heads, HEAD_WIDTH)
    cache_v = cache_v.reshape(depth * n_pool, page * heads, HEAD_WIDTH)

    def slot(i, j):
        return (i * gj + j) // n_chunks

    def page_spec(p):
        def index(i, j, pt):
            step = i * gj + j
            req = jnp.minimum(step // n_chunks, n_req - 1)
            return (layer * n_pool + pt[req * n_pages + (step % n_chunks) * pages + p], 0, 0)
        return pl.BlockSpec((1, page * heads, HEAD_WIDTH), index)

    tok_spec = pl.BlockSpec((1, n_tok, width), lambda i, j, pt: (jnp.minimum(slot(i, j), n_req - 1), 0, 0))
    specs = [
        pl.BlockSpec(memory_space=pltpu.SMEM),
        tok_spec, tok_spec, tok_spec,
        pl.BlockSpec((4, HEAD_DIM), lambda i, j, pt: (0, 0)),
        pl.BlockSpec((1, HEAD_WIDTH), lambda i, j, pt: (0, 0)),
    ] + [page_spec(p) for p in range(pages)] * 2
    args = [slopes, q, k_new, v_new, lam_vecs, subln.reshape(1, HEAD_WIDTH)] + [cache_k] * pages + [cache_v] * pages
    att_spec = pl.BlockSpec((1, n_tok, width), lambda i, j, pt: (slot(i, j), 0, 0))
    att_shape = jax.ShapeDtypeStruct((n_slots, n_tok, width), F32)
    scratch = [
        pltpu.VMEM((width, cols), BF16),
        pltpu.VMEM((chunk, cols), F32),
        pltpu.VMEM((1, cols), F32),
        pltpu.VMEM((1, cols), F32),
        pltpu.VMEM((1, cols), F32),
        pltpu.VMEM((cols, HEAD_WIDTH), F32),
        pltpu.VMEM((chunk, width), BF16),
        pltpu.VMEM((chunk, width), BF16),
    ]
    cfg = _Decode(n_req=n_req, n_chunks=n_chunks, pages=pages, heads=heads, past=n_pages * page,
                  lam_init=lam_init, value_parts=heads // (2 if heads % 2 == 0 else 1), n_inputs=len(specs))
    return cfg, page_table.reshape(-1), specs, args, att_spec, att_shape, scratch


def _conv_kernel(*refs, taps, rc, halo, has_hist):
    if has_hist:
        x_ref, h_ref, w_ref, cb_ref, g_ref, b_ref, o_ref, wb_ref = refs
    else:
        x_ref, w_ref, cb_ref, g_ref, b_ref, o_ref, wb_ref = refs
    seq, ch = x_ref.shape[1], x_ref.shape[2]
    lead = halo - (taps - 1)
    n_out = rc // SUBLANES
    n_win = n_out + halo // SUBLANES
    n_lb = ch // LANES

    for j in range(taps):
        wb_ref[j] = jnp.broadcast_to(w_ref[j:j + 1, :], (SUBLANES, ch))
    sublane = lax.broadcasted_iota(jnp.int32, (SUBLANES, LANES), 0)

    def conv_rows(load_tile, r0):
        parts = [[None] * n_lb for _ in range(n_out)]
        for lb in range(n_lb):
            ls = slice(lb * LANES, (lb + 1) * LANES)
            xv = [load_tile(i, ls) for i in range(n_win)]
            out = [None] * n_out
            for s in range(SUBLANES):
                a_vals = [a for a in range(n_win) if lead <= SUBLANES * a + s <= lead + taps - 1]
                if not a_vals:
                    continue
                part = [None] * (n_out + (1 if s else 0))
                for a in a_vals:
                    w = wb_ref[SUBLANES * a + s - lead, :, ls]
                    for m in range(len(part)):
                        term = xv[m + a] * w
                        part[m] = term if part[m] is None else part[m] + term
                for m in range(n_out):
                    if s == 0:
                        u = part[m]
                    else:
                        u = pltpu.roll(jnp.where(sublane >= s, part[m], part[m + 1]), SUBLANES - s, 0)
                    out[m] = u if out[m] is None else out[m] + u
            for m in range(n_out):
                parts[m][lb] = out[m]
        rows = []
        for m in range(n_out):
            y = jnp.concatenate(parts[m], axis=1) + cb_ref[...]
            mu = jnp.mean(y, axis=-1, keepdims=True)
            yc = y - mu
            var = jnp.mean(yc * yc, axis=-1, keepdims=True)
            z = yc * lax.rsqrt(var + EPS) * g_ref[...] + b_ref[...]
            rows.append(z * jax.nn.sigmoid(z))
        o_ref[0, pl.ds(r0, rc), :] = jnp.concatenate(rows, axis=0).astype(o_ref.dtype)

    n_static = min(-(-halo // rc), seq // rc)
    for r in range(n_static):
        def static_tile(i, ls, r=r):
            row = r * rc - halo + i * SUBLANES
            if row >= 0:
                return x_ref[0, row:row + SUBLANES, ls]
            if has_hist:
                return h_ref[0, halo + row:halo + row + SUBLANES, ls]
            return jnp.zeros((SUBLANES, LANES), F32)

        conv_rows(static_tile, r * rc)

    def chunk(r, carry):
        r0 = pl.multiple_of(r * rc, SUBLANES)

        def tile(i, ls):
            return x_ref[0, pl.ds(pl.multiple_of(r0 - halo + i * SUBLANES, SUBLANES), SUBLANES), ls]

        conv_rows(tile, r0)
        return carry

    if seq // rc > n_static:
        lax.fori_loop(n_static, seq // rc, chunk, 0)


def _conv_group(glu, state, conv_w, conv_b, norm_g, norm_b):
    batch, seq, ch = glu.shape
    taps = conv_w.shape[0]
    halo = -(-(taps - 1) // SUBLANES) * SUBLANES
    rc = _seq_tile(seq, 48)
    has_hist = state is not None
    out_dtype = BF16 if rc % BF16_ROWS == 0 else F32
    vec = pl.BlockSpec((1, ch), lambda b: (0, 0))
    in_specs = [pl.BlockSpec((1, seq, ch), lambda b: (b, 0, 0))]
    args = [glu]
    if has_hist:
        in_specs.append(pl.BlockSpec((1, halo, ch), lambda b: (b, 0, 0)))
        args.append(jnp.pad(state, ((0, 0), (halo - (taps - 1), 0), (0, 0))))
    in_specs += [pl.BlockSpec((taps, ch), lambda b: (0, 0)), vec, vec, vec]
    args += [conv_w, conv_b.reshape(1, ch), norm_g.reshape(1, ch), norm_b.reshape(1, ch)]
    return pl.pallas_call(
        functools.partial(_conv_kernel, taps=taps, rc=rc, halo=halo, has_hist=has_hist),
        grid=(batch,),
        in_specs=in_specs,
        out_specs=pl.BlockSpec((1, seq, ch), lambda b: (b, 0, 0)),
        out_shape=jax.ShapeDtypeStruct((batch, seq, ch), out_dtype),
        scratch_shapes=[pltpu.VMEM((taps, SUBLANES, ch), F32)],
        compiler_params=_params("parallel"),
        name="conv_group",
    )(*args)


def _mix_out_kernel(x_ref, a_ref, c_ref, wa_ref, wc_ref, o_ref):
    y = jnp.dot(a_ref[...].astype(BF16), wa_ref[...], preferred_element_type=F32)
    y += jnp.dot(c_ref[...].astype(BF16), wc_ref[...], preferred_element_type=F32)
    o_ref[...] = x_ref[...] + y


def _mix_out(x, att, conv, w_out):
    rows, d = x.shape
    wa = att.shape[1]
    wc = conv.shape[1]
    assert wa == wc and w_out.shape[0] == wa + wc
    tm = _row_tile(rows, 704)
    return pl.pallas_call(
        _mix_out_kernel,
        grid=(rows // tm,),
        in_specs=[
            pl.BlockSpec((tm, d), lambda i: (i, 0)),
            pl.BlockSpec((tm, wa), lambda i: (i, 0)),
            pl.BlockSpec((tm, wc), lambda i: (i, 0)),
            pl.BlockSpec((wa, d), lambda i: (0, 0)),
            pl.BlockSpec((wc, d), lambda i: (1, 0)),
        ],
        out_specs=pl.BlockSpec((tm, d), lambda i: (i, 0)),
        out_shape=jax.ShapeDtypeStruct((rows, d), F32),
        compiler_params=_params("parallel"),
        name="mix_out",
    )(x, att, conv, w_out, w_out)


def kernel(x_prompt, x_sample, cache_k, cache_v, state_conv, page_table, meta_tokens, ffn1_norm, ffn1_w_gate, ffn1_w_up, ffn1_w_down, mix_norm, w_in, lambda_q1, lambda_k1, lambda_q2, lambda_k2, attn_subln, conv_w, conv_b, conv_norm_g, conv_norm_b, w_out, ffn2_norm, ffn2_w_gate, ffn2_w_up, ffn2_w_down, final_norm):
    depth = w_in.shape[0]
    batch, seq0, d = x_prompt.shape
    n_meta = meta_tokens.shape[0]
    seq = seq0 + n_meta
    n_req, n_tok, _ = x_sample.shape
    taps = conv_w.shape[1]
    width = w_out.shape[1] // 2
    heads = width // HEAD_WIDTH
    slopes = jnp.array([2.0 ** (-8.0 * (i + 1) / heads) for i in range(heads)], F32)

    meta = jnp.broadcast_to(meta_tokens.astype(x_prompt.dtype)[None], (batch, n_meta, d))
    xp = jnp.concatenate([meta, x_prompt], axis=1).reshape(batch * seq, d)
    xs = x_sample.reshape(n_req * n_tok, d)

    outs = [[] for _ in range(6)]
    for l in range(depth):
        lam_init = 0.8 - 0.6 * math.exp(-0.3 * l)
        lam_vecs = jnp.stack([lambda_q1[l], lambda_k1[l], lambda_q2[l], lambda_k2[l]])
        wg2, wu2, wd2 = (w[l].astype(BF16) for w in (ffn2_w_gate, ffn2_w_up, ffn2_w_down))
        w_out_l = w_out[l].astype(BF16)
        last = l == depth - 1

        xs, wg1, wu1, wd1 = _ffn(xs, ffn1_norm[l], ffn1_w_gate[l], ffn1_w_up[l], ffn1_w_down[l])
        (qs, ks, vs, glus), w_in_l = _mix_in(xs, mix_norm[l], w_in[l], width)
        shape_s = (n_req, n_tok, width)
        qs, ks3, vs3, glus = (a.reshape(shape_s) for a in (qs, ks, vs, glus))
        half = -(-n_req // 2)

        def decode_args(lo, hi):
            return dict(q=qs[lo:hi], k_new=ks3[lo:hi], v_new=vs3[lo:hi], cache_k=cache_k, cache_v=cache_v,
                        layer=l, page_table=page_table[lo:hi], lam_vecs=lam_vecs, subln=attn_subln[l],
                        slopes=slopes, lam_init=lam_init)

        xp, att_s0 = _ffn(xp, ffn1_norm[l], wg1, wu1, wd1, decode_args=decode_args(0, half))

        qp, kp, vp, glup = _mix_in(xp, mix_norm[l], w_in_l, width)
        shape_p = (batch, seq, width)
        att_p = _attn_prompt(qp.reshape(shape_p), kp.reshape(shape_p), vp.reshape(shape_p),
                             lam_vecs, attn_subln[l], slopes, lam_init)
        glup = glup.reshape(shape_p)
        conv_p = _conv_group(glup, None, conv_w[l], conv_b[l], conv_norm_g[l], conv_norm_b[l])
        xp = _mix_out(xp, att_p.reshape(batch * seq, width), conv_p.reshape(batch * seq, width), w_out_l)
        outs[0].append(kp.reshape(batch, seq, heads, HEAD_WIDTH))
        outs[1].append(vp.reshape(batch, seq, heads, HEAD_WIDTH))
        outs[2].append(glup[:, seq - (taps - 1):])

        fg = final_norm if last else None
        xp, att_s1 = _ffn(xp, ffn2_norm[l], wg2, wu2, wd2, fg, decode_args=decode_args(half, n_req))

        att_s = jnp.concatenate([att_s0, att_s1], axis=0)
        conv_s = _conv_group(glus, state_conv[l], conv_w[l], conv_b[l], conv_norm_g[l], conv_norm_b[l])
        xs = _mix_out(xs, att_s.reshape(n_req * n_tok, width), conv_s.reshape(n_req * n_tok, width), w_out_l)
        outs[3].append(ks.reshape(n_req, n_tok, heads, HEAD_WIDTH))
        outs[4].append(vs.reshape(n_req, n_tok, heads, HEAD_WIDTH))
        outs[5].append(jnp.concatenate([state_conv[l], glus], axis=1)[:, -(taps - 1):])
        xs = _ffn(xs, ffn2_norm[l], wg2, wu2, wd2, fg)

    y_prompt = xp.reshape(batch, seq, d)[:, n_meta:]
    y_sample = xs.reshape(n_req, n_tok, d)
    return (y_prompt, y_sample) + tuple(jnp.stack(o) for o in outs)
```

```python
import functools
import math
from typing import NamedTuple

import jax
import jax.numpy as jnp
from jax import lax
from jax.experimental import pallas as pl
from jax.experimental.pallas import tpu as pltpu

EPS = 1e-6
HEAD_DIM = 64
HEAD_WIDTH = 2 * HEAD_DIM
LANES = 128
SUBLANES = 8
BF16_ROWS = 16
LOG2E = math.log2(math.e)
VMEM_LIMIT_BYTES = 56 * 1024 * 1024
VMEM_LIMIT_FUSED_BYTES = 61 * 1024 * 1024

BF16 = jnp.bfloat16
F32 = jnp.float32
_NT = (((1,), (1,)), ((), ()))


def _row_tile(rows, target):
    best = None
    for t in range(BF16_ROWS, min(rows, target) + 1, BF16_ROWS):
        if rows % t == 0:
            best = t
    assert best is not None, (rows, target)
    return best


def _col_tile(cols, target):
    best = None
    for t in range(LANES, min(cols, target) + 1, LANES):
        if cols % t == 0:
            best = t
    assert best is not None, (cols, target)
    return best


def _idiv(x, n):
    if n & (n - 1) == 0:
        return lax.shift_right_logical(x, n.bit_length() - 1)
    return x // n


def _imod(x, n):
    if n & (n - 1) == 0:
        return lax.bitwise_and(x, n - 1)
    return lax.rem(x, n)


def _rms(x, gain):
    return x * lax.rsqrt(jnp.mean(x * x, axis=-1, keepdims=True) + EPS) * gain


def _params(*semantics, vmem_limit_bytes=VMEM_LIMIT_BYTES):
    return pltpu.CompilerParams(dimension_semantics=semantics, vmem_limit_bytes=vmem_limit_bytes)


def _ffn_kernel(*refs, final, decode, emit_bf16, n_cast=0):
    if decode is not None:
        refs = refs[1:]
    x_ref, g_ref, wg_ref, wu_ref, wd_ref = refs[:5]
    pos = 5
    if decode is not None:
        dec_in = refs[pos:pos + decode.n_inputs]
        pos += decode.n_inputs
    if final:
        fg_ref = refs[pos]
        pos += 1
    cast_in = refs[pos:pos + n_cast]
    pos += n_cast
    o_ref = refs[pos]
    pos += 1
    if decode is not None:
        att_ref = refs[pos]
        pos += 1
    for w_ref, wb_ref in zip(cast_in, refs[pos:pos + n_cast]):
        wb_ref[...] = w_ref[...].astype(BF16)
    pos += n_cast
    if emit_bf16:
        wgb_ref, wub_ref, wdb_ref = refs[pos:pos + 3]
        pos += 3
    h_ref = refs[pos]
    dec_scratch = refs[pos + 1:]
    j = pl.program_id(1)

    @pl.when(j == 0)
    def _():
        h_ref[...] = _rms(x_ref[...], g_ref[...]).astype(BF16)
        o_ref[...] = jnp.zeros_like(o_ref)

    if decode is not None:
        step = pl.program_id(0) * pl.num_programs(1) + j
        c = lax.rem(step, decode.n_chunks)
        dec_start, dec_scores, dec_values, dec_finish = _decode_phases(c, decode, dec_in, att_ref, dec_scratch)
        pl.when(c == 0)(dec_start)

    h = h_ref[...]
    wg, wu, wd = wg_ref[...], wu_ref[...], wd_ref[...]
    if emit_bf16:
        wg, wu, wd = wg.astype(BF16), wu.astype(BF16), wd.astype(BF16)
        wgb_ref[...], wub_ref[...], wdb_ref[...] = wg, wu, wd
    gate = jnp.dot(h, wg, preferred_element_type=F32)
    if decode is not None:
        dec_state = dec_scores()
    up = jnp.dot(h, wu, preferred_element_type=F32)
    if decode is not None:
        for part in range(decode.value_parts):
            dec_values(part, *dec_state)
    act = (gate * jax.nn.sigmoid(gate) * up).astype(BF16)
    o_ref[...] += jnp.dot(act, wd, preferred_element_type=F32)

    if decode is not None:
        pl.when(c == decode.n_chunks - 1)(dec_finish)

    @pl.when(j == pl.num_programs(1) - 1)
    def _():
        y = x_ref[...] + 0.5 * o_ref[...]
        if final:
            y = _rms(y, fg_ref[...])
        o_ref[...] = y


def _ffn(x, gain, w_gate, w_up, w_down, final_gain=None, decode_args=None, cast_weights=()):
    rows, d = x.shape
    f = w_gate.shape[1]
    tm = _row_tile(rows, 704)
    tf = _col_tile(f, 512 if decode_args is None else 256)
    grid = (rows // tm, f // tf)
    final = final_gain is not None
    in_specs = [
        pl.BlockSpec((tm, d), lambda i, j, *_: (i, 0)),
        pl.BlockSpec((1, d), lambda i, j, *_: (0, 0)),
        pl.BlockSpec((d, tf), lambda i, j, *_: (0, j)),
        pl.BlockSpec((d, tf), lambda i, j, *_: (0, j)),
        pl.BlockSpec((tf, d), lambda i, j, *_: (j, 0)),
    ]
    args = [x, gain.reshape(1, d), w_gate, w_up, w_down]
    final_spec = [pl.BlockSpec((1, d), lambda i, j, *_: (0, 0))] if final else []
    final_arg = [final_gain.reshape(1, d)] if final else []
    out_specs = pl.BlockSpec((tm, d), lambda i, j, *_: (i, 0))
    out_shape = jax.ShapeDtypeStruct((rows, d), F32)
    scratch = [pltpu.VMEM((tm, d), BF16)]
    name = "half_ffn_final" if final else "half_ffn"
    emit_bf16 = w_gate.dtype != BF16
    if emit_bf16:
        assert decode_args is None and grid[0] == 1
        return pl.pallas_call(
            functools.partial(_ffn_kernel, final=final, decode=None, emit_bf16=True),
            grid=grid, in_specs=in_specs + final_spec,
            out_specs=[out_specs] + in_specs[2:5],
            out_shape=[out_shape] + [jax.ShapeDtypeStruct(w.shape, BF16) for w in (w_gate, w_up, w_down)],
            scratch_shapes=scratch, compiler_params=_params("arbitrary", "arbitrary"), name=name + "_cast",
        )(*args, *final_arg)
    if decode_args is None:
        return pl.pallas_call(
            functools.partial(_ffn_kernel, final=final, decode=None, emit_bf16=False),
            grid=grid, in_specs=in_specs + final_spec, out_specs=out_specs, out_shape=out_shape,
            scratch_shapes=scratch, compiler_params=_params("parallel", "arbitrary"), name=name,
        )(*args, *final_arg)

    decode, page_table, dec_specs, dec_args, att_spec, att_shape, dec_scratch = _decode_plan(
        grid, **decode_args)
    cast_specs = []
    nj = grid[1]
    for w in cast_weights:
        rows_w, cols_w = w.shape
        assert rows_w % tf == 0 and cols_w % tf == 0 and nj in (rows_w // tf, cols_w // tf), w.shape
        along_cols = cols_w // tf == nj
        n_i = rows_w // tf if along_cols else cols_w // tf
        assert n_i <= grid[0], (w.shape, grid)

        def index(i, j, *_, n_i=n_i, along_cols=along_cols):
            done = i >= n_i
            a, b = jnp.where(done, n_i - 1, i), jnp.where(done, nj - 1, j)
            return (a, b) if along_cols else (b, a)

        cast_specs.append(pl.BlockSpec((tf, tf), index))
    grid_spec = pltpu.PrefetchScalarGridSpec(
        num_scalar_prefetch=1, grid=grid,
        in_specs=in_specs + dec_specs + final_spec + cast_specs,
        out_specs=[out_specs, att_spec] + cast_specs,
        scratch_shapes=scratch + dec_scratch,
    )
    y, att, *casts = pl.pallas_call(
        functools.partial(_ffn_kernel, final=final, decode=decode, emit_bf16=False, n_cast=len(cast_weights)),
        grid_spec=grid_spec,
        out_shape=[out_shape, att_shape] + [jax.ShapeDtypeStruct(w.shape, BF16) for w in cast_weights],
        compiler_params=_params("arbitrary", "arbitrary", vmem_limit_bytes=VMEM_LIMIT_FUSED_BYTES),
        name=name + "_decode",
    )(page_table, *args, *dec_args, *final_arg, *cast_weights)
    return y, att[:decode.n_req], casts


def _mix_in_kernel(x_ref, g_ref, wa_ref, wg_ref, q_ref, k_ref, v_ref, glu_ref, *rest, emit_bf16):
    if emit_bf16:
        wab_ref, wgb_ref, h_ref = rest
    else:
        (h_ref,) = rest
    j = pl.program_id(1)

    @pl.when(j == 0)
    def _():
        h_ref[...] = _rms(x_ref[...], g_ref[...]).astype(BF16)

    h = h_ref[...]
    wa = wa_ref[...]
    if emit_bf16:
        wa = wa.astype(BF16)
        wab_ref[...] = wa
    r = jnp.dot(h, wa, preferred_element_type=F32)

    @pl.when(j == 0)
    def _():
        q_ref[...] = r * (HEAD_DIM ** -0.5)

    @pl.when(j == 1)
    def _():
        k_ref[...] = r

    @pl.when(j == 2)
    def _():
        v_ref[...] = r

    @pl.when(j == 3)
    def _():
        wg = wg_ref[...]
        if emit_bf16:
            wg = wg.astype(BF16)
            wgb_ref[...] = wg
        gate = jnp.dot(h, wg, preferred_element_type=F32)
        glu_ref[...] = r * jax.nn.sigmoid(gate)


def _mix_in(x, gain, w_in, width):
    rows, d = x.shape
    tm = _row_tile(rows, 704)
    grid = (rows // tm, 4)
    out_spec = pl.BlockSpec((tm, width), lambda i, j: (i, 0))
    out_sds = jax.ShapeDtypeStruct((rows, width), F32)
    main_spec = pl.BlockSpec((d, width), lambda i, j: (0, j))
    emit_bf16 = not isinstance(w_in, tuple)
    if emit_bf16:
        assert w_in.shape[1] == 5 * width and grid[0] == 1
        weights = (w_in, w_in)
        gate_spec = pl.BlockSpec((d, width), lambda i, j: (0, 4))
        extra_specs = [main_spec, pl.BlockSpec((d, width), lambda i, j: (0, 0))]
        extra_shapes = [jax.ShapeDtypeStruct((d, 4 * width), BF16), jax.ShapeDtypeStruct((d, width), BF16)]
    else:
        weights = w_in
        gate_spec = pl.BlockSpec((d, width), lambda i, j: (0, 0))
        extra_specs, extra_shapes = [], []
    outs = pl.pallas_call(
        functools.partial(_mix_in_kernel, emit_bf16=emit_bf16),
        grid=grid,
        in_specs=[
            pl.BlockSpec((tm, d), lambda i, j: (i, 0)),
            pl.BlockSpec((1, d), lambda i, j: (0, 0)),
            main_spec,
            gate_spec,
        ],
        out_specs=[out_spec] * 4 + extra_specs,
        out_shape=[out_sds] * 4 + extra_shapes,
        scratch_shapes=[pltpu.VMEM((tm, d), BF16)],
        compiler_params=_params("arbitrary" if emit_bf16 else "parallel", "arbitrary"),
        name="mix_in_cast" if emit_bf16 else "mix_in",
    )(x, gain.reshape(1, d), *weights)
    if emit_bf16:
        return outs[:4], (outs[4], outs[5])
    return outs


def _diff_lambda(lam_ref, lam_init):
    lv = lam_ref[...]
    d1 = jnp.sum(lv[0:1] * lv[1:2], axis=-1, keepdims=True)
    d2 = jnp.sum(lv[2:3] * lv[3:4], axis=-1, keepdims=True)
    return jnp.exp(d1) - jnp.exp(d2) + lam_init


def _attn_prompt_kernel(slopes_ref, q_ref, k_ref, v_ref, lam_ref, sub_ref, o_ref,
                        k1_ref, k2_ref, vb_ref, dist_ref, sd_ref, m_ref, l_ref, acc_ref,
                        *, seq, tq, tk, hps, lam_init):
    head0 = pl.program_id(1) * hps
    seq_pad = vb_ref.shape[1]
    k_refs = (k1_ref, k2_ref)

    dist = (lax.broadcasted_iota(jnp.int32, (tq, tk), 1)
            - lax.broadcasted_iota(jnp.int32, (tq, tk), 0))
    dist_ref[...] = dist
    slope2 = []
    for hh in range(hps):
        cols = slice(hh * HEAD_WIDTH, (hh + 1) * HEAD_WIDTH)
        k = k_ref[0, :, cols]
        k1_ref[hh, 0:seq, :] = k[:, :HEAD_DIM].astype(BF16)
        k2_ref[hh, 0:seq, :] = k[:, HEAD_DIM:].astype(BF16)
        vb_ref[hh, 0:seq, :] = v_ref[0, :, cols].astype(BF16)
        if seq_pad > seq:
            k1_ref[hh, seq:seq_pad, :] = jnp.zeros((seq_pad - seq, HEAD_DIM), BF16)
            k2_ref[hh, seq:seq_pad, :] = jnp.zeros((seq_pad - seq, HEAD_DIM), BF16)
            vb_ref[hh, seq:seq_pad, :] = jnp.zeros((seq_pad - seq, HEAD_WIDTH), BF16)
        slope2.append(slopes_ref[head0 + hh] * LOG2E)
        sd_ref[hh] = slope2[hh] * dist.astype(F32)
    lam = _diff_lambda(lam_ref, lam_init)
    sub_gain = sub_ref[...] * (1.0 - lam_init)
    stat = (tq, LANES)

    def q_tile(i, carry):
        q0 = pl.multiple_of(i * tq, BF16_ROWS)
        qt = q_ref[0, pl.ds(q0, tq), :] * LOG2E
        qs = [qt[:, n * HEAD_DIM:(n + 1) * HEAD_DIM].astype(BF16) for n in range(2 * hps)]
        m_ref[...] = jnp.full(m_ref.shape, -jnp.inf, F32)
        l_ref[...] = jnp.zeros_like(l_ref)
        acc_ref[...] = jnp.zeros_like(acc_ref)

        def kv_tile(j, carry, masked):
            k0 = pl.multiple_of(j * tk, tk)
            off = q0 - k0
            for hh in range(hps):
                vt = vb_ref[hh, pl.ds(k0, tk), :]
                tile_bias = -slope2[hh] * off.astype(F32)
                for mp in range(2):
                    n = 2 * hh + mp
                    kmp = k_refs[mp][hh, pl.ds(k0, tk), :]
                    s = lax.dot_general(qs[n], kmp, _NT, preferred_element_type=F32) + sd_ref[hh]
                    if masked:
                        s = jnp.where(dist_ref[...] <= off, s, -jnp.inf)
                    m_old = m_ref[n]
                    m_tile = jnp.broadcast_to(jnp.max(s, axis=-1, keepdims=True), stat) + tile_bias
                    m_new = jnp.maximum(m_old, m_tile)
                    corr = jnp.exp2(m_old - m_new)
                    shift = tile_bias - m_new
                    wide = jnp.concatenate([shift] * -(-tk // LANES), axis=1)[:, :tk]
                    p = jnp.exp2(s + wide)
                    l_ref[n] = l_ref[n] * corr + jnp.broadcast_to(
                        jnp.sum(p, axis=-1, keepdims=True), stat)
                    acc_ref[n] = acc_ref[n] * corr + jnp.dot(
                        p.astype(BF16), vt, preferred_element_type=F32)
                    m_ref[n] = m_new
            return carry

        n_full = q0 // tk
        n_all = (q0 + tq + tk - 1) // tk
        lax.fori_loop(0, n_full, functools.partial(kv_tile, masked=False), 0)
        lax.fori_loop(n_full, n_all, functools.partial(kv_tile, masked=True), 0)

        for hh in range(hps):
            o1 = acc_ref[2 * hh] / l_ref[2 * hh]
            o2 = acc_ref[2 * hh + 1] / l_ref[2 * hh + 1]
            o_ref[0, pl.ds(q0, tq), hh * HEAD_WIDTH:(hh + 1) * HEAD_WIDTH] = _rms(
                o1 - lam * o2, sub_gain).astype(o_ref.dtype)
        return carry

    lax.fori_loop(0, seq // tq, q_tile, 0)


def _seq_tile(seq, target):
    best = None
    for t in range(SUBLANES, min(seq, target) + 1, SUBLANES):
        if seq % t == 0:
            best = t
    assert best is not None, (seq, target)
    return best


def _attn_prompt(q, k, v, lam_vecs, subln, slopes, lam_init):
    batch, seq, width = q.shape
    heads = width // HEAD_WIDTH
    assert HEAD_WIDTH == LANES
    hps = 2 if heads % 2 == 0 else 1
    tq = _row_tile(seq, 704)
    tk = tq
    seq_pad = -(-seq // tk) * tk
    blk = pl.BlockSpec((1, seq, hps * HEAD_WIDTH), lambda b, h: (b, 0, h))
    return pl.pallas_call(
        functools.partial(_attn_prompt_kernel, seq=seq, tq=tq, tk=tk, hps=hps, lam_init=lam_init),
        grid=(batch, heads // hps),
        in_specs=[
            pl.BlockSpec(memory_space=pltpu.SMEM),
            blk, blk, blk,
            pl.BlockSpec((4, HEAD_DIM), lambda b, h: (0, 0)),
            pl.BlockSpec((1, HEAD_WIDTH), lambda b, h: (0, 0)),
        ],
        out_specs=blk,
        out_shape=jax.ShapeDtypeStruct((batch, seq, width), BF16),
        scratch_shapes=[
            pltpu.VMEM((hps, seq_pad, HEAD_DIM), BF16),
            pltpu.VMEM((hps, seq_pad, HEAD_DIM), BF16),
            pltpu.VMEM((hps, seq_pad, HEAD_WIDTH), BF16),
            pltpu.VMEM((tq, tk), jnp.int32),
            pltpu.VMEM((hps, tq, tk), F32),
            pltpu.VMEM((2 * hps, tq, LANES), F32),
            pltpu.VMEM((2 * hps, tq, LANES), F32),
            pltpu.VMEM((2 * hps, tq, HEAD_WIDTH), F32),
        ],
        compiler_params=_params("parallel", "parallel"),
        name="attn_prompt",
    )(slopes, q, k, v, lam_vecs, subln.reshape(1, HEAD_WIDTH))


class _Decode(NamedTuple):
    n_req: int
    n_chunks: int
    pages: int
    heads: int
    past: int
    lam_init: float
    value_parts: int
    n_inputs: int


def _decode_phases(c, cfg, dec_in, o_ref, scratch):
    pages, heads, past, lam_init = cfg.pages, cfg.heads, cfg.past, cfg.lam_init
    slopes_ref, q_ref, kn_ref, vn_ref, lam_ref, sub_ref = dec_in[:6]
    k_refs = dec_in[6:6 + pages]
    v_refs = dec_in[6 + pages:6 + 2 * pages]
    wq_ref, b0_ref, sl_ref, m_ref, l_ref, acc_ref, kc_ref, vc_ref = scratch
    n_tok = q_ref.shape[1]
    width, cols = wq_ref.shape
    group = 2 * n_tok
    page = k_refs[0].shape[1] // heads
    chunk = pages * page

    def head_group(h):
        return slice(h * group, (h + 1) * group)

    def head_cols(h):
        return slice(h * HEAD_WIDTH, (h + 1) * HEAD_WIDTH)

    def as_rows(stat):
        return jnp.broadcast_to(stat, (HEAD_WIDTH, cols)).T

    def start():
        o_ref[...] = jnp.zeros_like(o_ref)
        rep = jnp.concatenate([q_ref[0]] * (cols // n_tok), axis=0)
        r_io = lax.broadcasted_iota(jnp.int32, (cols, width), 0)
        c_io = lax.broadcasted_iota(jnp.int32, (cols, width), 1)
        keep = _idiv(c_io, HEAD_DIM) == _idiv(r_io, n_tok)
        wq_ref[...] = jnp.where(keep, rep, 0.0).T.astype(BF16)
        col = lax.broadcasted_iota(jnp.int32, (1, cols), 1)
        slope_row = jnp.zeros((1, cols), F32)
        for h in range(heads):
            slope_row = jnp.where(_idiv(col, group) == h, slopes_ref[h], slope_row)
        sl_ref[...] = slope_row
        key = lax.broadcasted_iota(jnp.int32, (chunk, cols), 0)
        tok = _imod(lax.broadcasted_iota(jnp.int32, (chunk, cols), 1), n_tok)
        b0_ref[...] = slope_row * (key - tok).astype(F32)
        m_ref[...] = jnp.full(m_ref.shape, -jnp.inf, F32)
        l_ref[...] = jnp.zeros_like(l_ref)
        acc_ref[...] = jnp.zeros_like(acc_ref)

    def softmax_part(s):
        m_old = m_ref[...]
        m_new = jnp.maximum(m_old, jnp.max(s, axis=0, keepdims=True))
        corr = jnp.exp(m_old - m_new)
        p = jnp.exp(s - m_new)
        l_ref[...] = l_ref[...] * corr + jnp.sum(p, axis=0, keepdims=True)
        m_ref[...] = m_new
        return p.T.astype(BF16), as_rows(corr)

    span = 2 if heads % 2 == 0 else 1

    def values_part(part, pb, corr_rows, n_keys):
        h0 = part * span
        z = jnp.dot(pb[h0 * group:(h0 + span) * group],
                    vc_ref[0:n_keys, h0 * HEAD_WIDTH:(h0 + span) * HEAD_WIDTH],
                    preferred_element_type=F32)
        for i in range(span):
            hg = head_group(h0 + i)
            acc_ref[hg, :] = acc_ref[hg, :] * corr_rows[hg] + z[i * group:(i + 1) * group,
                                                                i * HEAD_WIDTH:(i + 1) * HEAD_WIDTH]

    def scores(n_keys):
        return jnp.dot(kc_ref[0:n_keys, :], wq_ref[...], preferred_element_type=F32) + b0_ref[0:n_keys, :]

    def regroup(page_refs, dst_ref, head_range):
        for p in range(pages):
            for t in range(0, page, BF16_ROWS):
                for h in head_range:
                    src = pl.ds(t * heads + h, BF16_ROWS, stride=heads)
                    dst = slice(p * page + t, p * page + t + BF16_ROWS)
                    dst_ref[dst, head_cols(h)] = page_refs[p][0, src, :].astype(BF16)

    def chunk_scores():
        regroup(k_refs, kc_ref, range(heads))
        chunk_shift = sl_ref[...] * (c * chunk - past).astype(F32)
        return softmax_part(scores(chunk) + chunk_shift)

    def chunk_values(part, pb, corr_rows):
        regroup(v_refs, vc_ref, range(part * span, (part + 1) * span))
        values_part(part, pb, corr_rows, chunk)

    def finish():
        pad = jnp.zeros((page - n_tok, width), F32)
        kc_ref[0:page, :] = jnp.concatenate([kn_ref[0], pad], axis=0).astype(BF16)
        vc_ref[0:page, :] = jnp.concatenate([vn_ref[0], pad], axis=0).astype(BF16)
        key = lax.broadcasted_iota(jnp.int32, (page, cols), 0)
        tok = _imod(lax.broadcasted_iota(jnp.int32, (page, cols), 1), n_tok)
        new_state = softmax_part(jnp.where(key <= tok, scores(page), -jnp.inf))
        for part in range(heads // span):
            values_part(part, *new_state, page)

        lam = _diff_lambda(lam_ref, lam_init)
        sub_gain = sub_ref[...] * (1.0 - lam_init)
        l_rows = as_rows(l_ref[...])
        for h in range(heads):
            hg = head_group(h)
            blk = acc_ref[hg, :] / l_rows[hg]
            o_ref[0, :, head_cols(h)] = _rms(blk[0:n_tok] - lam * blk[n_tok:group], sub_gain)

    return start, chunk_scores, chunk_values, finish


def _decode_plan(grid, q, k_new, v_new, cache_k, cache_v, layer, page_table, lam_vecs, subln, slopes, lam_init):
    n_req, n_tok, width = q.shape
    n_pages = page_table.shape[1]
    depth, n_pool, page, heads, _ = cache_k.shape
    group = 2 * n_tok
    assert group % BF16_ROWS == 0 and page % BF16_ROWS == 0 and n_tok <= page
    cols = heads * group
    pages = 8
    while n_pages % pages:
        pages //= 2
    chunk = pages * page
    n_chunks = n_pages // pages
    gi, gj = grid
    assert gi * gj >= n_req * n_chunks, (grid, n_req, n_chunks)
    n_slots = -(-(gi * gj) // n_chunks)
    cache_k = cache_k.reshape(depth * n_pool, page * heads, HEAD_WIDTH)
    cache_v = cache_v.reshape(depth * n_pool, page * heads, HEAD_WIDTH)

    def slot(i, j):
        return (i * gj + j) // n_chunks

    def page_spec(p):
        def index(i, j, pt):
            step = i * gj + j
            req = jnp.minimum(step // n_chunks, n_req - 1)
            return (layer * n_pool + pt[req * n_pages + (step % n_chunks) * pages + p], 0, 0)
        return pl.BlockSpec((1, page * heads, HEAD_WIDTH), index)

    tok_spec = pl.BlockSpec((1, n_tok, width), lambda i, j, pt: (jnp.minimum(slot(i, j), n_req - 1), 0, 0))
    specs = [
        pl.BlockSpec(memory_space=pltpu.SMEM),
        tok_spec, tok_spec, tok_spec,
        pl.BlockSpec((4, HEAD_DIM), lambda i, j, pt: (0, 0)),
        pl.BlockSpec((1, HEAD_WIDTH), lambda i, j, pt: (0, 0)),
    ] + [page_spec(p) for p in range(pages)] * 2
    args = [slopes, q, k_new, v_new, lam_vecs, subln.reshape(1, HEAD_WIDTH)] + [cache_k] * pages + [cache_v] * pages
    att_spec = pl.BlockSpec((1, n_tok, width), lambda i, j, pt: (slot(i, j), 0, 0))
    att_shape = jax.ShapeDtypeStruct((n_slots, n_tok, width), F32)
    scratch = [
        pltpu.VMEM((width, cols), BF16),
        pltpu.VMEM((chunk, cols), F32),
        pltpu.VMEM((1, cols), F32),
        pltpu.VMEM((1, cols), F32),
        pltpu.VMEM((1, cols), F32),
        pltpu.VMEM((cols, HEAD_WIDTH), F32),
        pltpu.VMEM((chunk, width), BF16),
        pltpu.VMEM((chunk, width), BF16),
    ]
    cfg = _Decode(n_req=n_req, n_chunks=n_chunks, pages=pages, heads=heads, past=n_pages * page,
                  lam_init=lam_init, value_parts=heads // (2 if heads % 2 == 0 else 1), n_inputs=len(specs))
    return cfg, page_table.reshape(-1), specs, args, att_spec, att_shape, scratch


def _conv_kernel(*refs, taps, rc, halo, has_hist):
    if has_hist:
        x_ref, h_ref, w_ref, cb_ref, g_ref, b_ref, o_ref, wb_ref = refs
    else:
        x_ref, w_ref, cb_ref, g_ref, b_ref, o_ref, wb_ref = refs
    seq, ch = x_ref.shape[1], x_ref.shape[2]
    lead = halo - (taps - 1)
    n_out = rc // SUBLANES
    n_win = n_out + halo // SUBLANES
    n_lb = ch // LANES

    for j in range(taps):
        wb_ref[j] = jnp.broadcast_to(w_ref[j:j + 1, :], (SUBLANES, ch))
    sublane = lax.broadcasted_iota(jnp.int32, (SUBLANES, LANES), 0)

    def conv_rows(load_tile, r0):
        parts = [[None] * n_lb for _ in range(n_out)]
        for lb in range(n_lb):
            ls = slice(lb * LANES, (lb + 1) * LANES)
            xv = [load_tile(i, ls) for i in range(n_win)]
            out = [None] * n_out
            for s in range(SUBLANES):
                a_vals = [a for a in range(n_win) if lead <= SUBLANES * a + s <= lead + taps - 1]
                if not a_vals:
                    continue
                part = [None] * (n_out + (1 if s else 0))
                for a in a_vals:
                    w = wb_ref[SUBLANES * a + s - lead, :, ls]
                    for m in range(len(part)):
                        term = xv[m + a] * w
                        part[m] = term if part[m] is None else part[m] + term
                for m in range(n_out):
                    if s == 0:
                        u = part[m]
                    else:
                        u = pltpu.roll(jnp.where(sublane >= s, part[m], part[m + 1]), SUBLANES - s, 0)
                    out[m] = u if out[m] is None else out[m] + u
            for m in range(n_out):
                parts[m][lb] = out[m]
        rows = []
        for m in range(n_out):
            y = jnp.concatenate(parts[m], axis=1) + cb_ref[...]
            mu = jnp.mean(y, axis=-1, keepdims=True)
            yc = y - mu
            var = jnp.mean(yc * yc, axis=-1, keepdims=True)
            z = yc * lax.rsqrt(var + EPS) * g_ref[...] + b_ref[...]
            rows.append(z * jax.nn.sigmoid(z))
        o_ref[0, pl.ds(r0, rc), :] = jnp.concatenate(rows, axis=0).astype(o_ref.dtype)

    n_static = min(-(-halo // rc), seq // rc)
    for r in range(n_static):
        def static_tile(i, ls, r=r):
            row = r * rc - halo + i * SUBLANES
            if row >= 0:
                return x_ref[0, row:row + SUBLANES, ls]
            if has_hist:
                return h_ref[0, halo + row:halo + row + SUBLANES, ls]
            return jnp.zeros((SUBLANES, LANES), F32)

        conv_rows(static_tile, r * rc)

    def chunk(r, carry):
        r0 = pl.multiple_of(r * rc, SUBLANES)

        def tile(i, ls):
            return x_ref[0, pl.ds(pl.multiple_of(r0 - halo + i * SUBLANES, SUBLANES), SUBLANES), ls]

        conv_rows(tile, r0)
        return carry

    if seq // rc > n_static:
        lax.fori_loop(n_static, seq // rc, chunk, 0)


def _conv_group(glu, state, conv_w, conv_b, norm_g, norm_b):
    batch, seq, ch = glu.shape
    taps = conv_w.shape[0]
    halo = -(-(taps - 1) // SUBLANES) * SUBLANES
    rc = _seq_tile(seq, 48)
    has_hist = state is not None
    out_dtype = BF16 if rc % BF16_ROWS == 0 else F32
    vec = pl.BlockSpec((1, ch), lambda b: (0, 0))
    in_specs = [pl.BlockSpec((1, seq, ch), lambda b: (b, 0, 0))]
    args = [glu]
    if has_hist:
        in_specs.append(pl.BlockSpec((1, halo, ch), lambda b: (b, 0, 0)))
        args.append(jnp.pad(state, ((0, 0), (halo - (taps - 1), 0), (0, 0))))
    in_specs += [pl.BlockSpec((taps, ch), lambda b: (0, 0)), vec, vec, vec]
    args += [conv_w, conv_b.reshape(1, ch), norm_g.reshape(1, ch), norm_b.reshape(1, ch)]
    return pl.pallas_call(
        functools.partial(_conv_kernel, taps=taps, rc=rc, halo=halo, has_hist=has_hist),
        grid=(batch,),
        in_specs=in_specs,
        out_specs=pl.BlockSpec((1, seq, ch), lambda b: (b, 0, 0)),
        out_shape=jax.ShapeDtypeStruct((batch, seq, ch), out_dtype),
        scratch_shapes=[pltpu.VMEM((taps, SUBLANES, ch), F32)],
        compiler_params=_params("parallel"),
        name="conv_group",
    )(*args)


def _mix_out_kernel(x_ref, a_ref, c_ref, wa_ref, wc_ref, o_ref):
    y = jnp.dot(a_ref[...].astype(BF16), wa_ref[...], preferred_element_type=F32)
    y += jnp.dot(c_ref[...].astype(BF16), wc_ref[...], preferred_element_type=F32)
    o_ref[...] = x_ref[...] + y


def _mix_out(x, att, conv, w_out):
    rows, d = x.shape
    wa = att.shape[1]
    wc = conv.shape[1]
    assert wa == wc and w_out.shape[0] == wa + wc
    tm = _row_tile(rows, 704)
    return pl.pallas_call(
        _mix_out_kernel,
        grid=(rows // tm,),
        in_specs=[
            pl.BlockSpec((tm, d), lambda i: (i, 0)),
            pl.BlockSpec((tm, wa), lambda i: (i, 0)),
            pl.BlockSpec((tm, wc), lambda i: (i, 0)),
            pl.BlockSpec((wa, d), lambda i: (0, 0)),
            pl.BlockSpec((wc, d), lambda i: (1, 0)),
        ],
        out_specs=pl.BlockSpec((tm, d), lambda i: (i, 0)),
        out_shape=jax.ShapeDtypeStruct((rows, d), F32),
        compiler_params=_params("parallel"),
        name="mix_out",
    )(x, att, conv, w_out, w_out)


def kernel(x_prompt, x_sample, cache_k, cache_v, state_conv, page_table, meta_tokens, ffn1_norm, ffn1_w_gate, ffn1_w_up, ffn1_w_down, mix_norm, w_in, lambda_q1, lambda_k1, lambda_q2, lambda_k2, attn_subln, conv_w, conv_b, conv_norm_g, conv_norm_b, w_out, ffn2_norm, ffn2_w_gate, ffn2_w_up, ffn2_w_down, final_norm):
    depth = w_in.shape[0]
    batch, seq0, d = x_prompt.shape
    n_meta = meta_tokens.shape[0]
    seq = seq0 + n_meta
    n_req, n_tok, _ = x_sample.shape
    taps = conv_w.shape[1]
    width = w_out.shape[1] // 2
    heads = width // HEAD_WIDTH
    slopes = jnp.array([2.0 ** (-8.0 * (i + 1) / heads) for i in range(heads)], F32)

    meta = jnp.broadcast_to(meta_tokens.astype(x_prompt.dtype)[None], (batch, n_meta, d))
    xp = jnp.concatenate([meta, x_prompt], axis=1).reshape(batch * seq, d)
    xs = x_sample.reshape(n_req * n_tok, d)

    outs = [[] for _ in range(6)]
    for l in range(depth):
        lam_init = 0.8 - 0.6 * math.exp(-0.3 * l)
        lam_vecs = jnp.stack([lambda_q1[l], lambda_k1[l], lambda_q2[l], lambda_k2[l]])
        w_out_l = w_out[l].astype(BF16)
        last = l == depth - 1

        xs, wg1, wu1, wd1 = _ffn(xs, ffn1_norm[l], ffn1_w_gate[l], ffn1_w_up[l], ffn1_w_down[l])
        (qs, ks, vs, glus), w_in_l = _mix_in(xs, mix_norm[l], w_in[l], width)
        shape_s = (n_req, n_tok, width)
        qs, ks3, vs3, glus = (a.reshape(shape_s) for a in (qs, ks, vs, glus))
        half = -(-n_req // 2)

        def decode_args(lo, hi):
            return dict(q=qs[lo:hi], k_new=ks3[lo:hi], v_new=vs3[lo:hi], cache_k=cache_k, cache_v=cache_v,
                        layer=l, page_table=page_table[lo:hi], lam_vecs=lam_vecs, subln=attn_subln[l],
                        slopes=slopes, lam_init=lam_init)

        xp, att_s0, (wg2, wu2, wd2) = _ffn(
            xp, ffn1_norm[l], wg1, wu1, wd1, decode_args=decode_args(0, half),
            cast_weights=(ffn2_w_gate[l], ffn2_w_up[l], ffn2_w_down[l]))

        qp, kp, vp, glup = _mix_in(xp, mix_norm[l], w_in_l, width)
        shape_p = (batch, seq, width)
        att_p = _attn_prompt(qp.reshape(shape_p), kp.reshape(shape_p), vp.reshape(shape_p),
                             lam_vecs, attn_subln[l], slopes, lam_init)
        glup = glup.reshape(shape_p)
        conv_p = _conv_group(glup, None, conv_w[l], conv_b[l], conv_norm_g[l], conv_norm_b[l])
        xp = _mix_out(xp, att_p.reshape(batch * seq, width), conv_p.reshape(batch * seq, width), w_out_l)
        outs[0].append(kp.reshape(batch, seq, heads, HEAD_WIDTH))
        outs[1].append(vp.reshape(batch, seq, heads, HEAD_WIDTH))
        outs[2].append(glup[:, seq - (taps - 1):])

        fg = final_norm if last else None
        xp, att_s1, _ = _ffn(xp, ffn2_norm[l], wg2, wu2, wd2, fg, decode_args=decode_args(half, n_req))

        att_s = jnp.concatenate([att_s0, att_s1], axis=0)
        conv_s = _conv_group(glus, state_conv[l], conv_w[l], conv_b[l], conv_norm_g[l], conv_norm_b[l])
        xs = _mix_out(xs, att_s.reshape(n_req * n_tok, width), conv_s.reshape(n_req * n_tok, width), w_out_l)
        outs[3].append(ks.reshape(n_req, n_tok, heads, HEAD_WIDTH))
        outs[4].append(vs.reshape(n_req, n_tok, heads, HEAD_WIDTH))
        outs[5].append(jnp.concatenate([state_conv[l], glus], axis=1)[:, -(taps - 1):])
        xs = _ffn(xs, ffn2_norm[l], wg2, wu2, wd2, fg)

    y_prompt = xp.reshape(batch, seq, d)[:, n_meta:]
    y_sample = xs.reshape(n_req, n_tok, d)
    return (y_prompt, y_sample) + tuple(jnp.stack(o) for o in outs)
```
